```python
import jax
import jax.numpy as jnp
from jax import lax
import numpy as np

D_MODEL = 1024
BATCH = 16
SEQ = 4096
DEPTH = 2

RWKV_HEAD = 64
RWKV_WIDTH = D_MODEL
RWKV_HEADS = RWKV_WIDTH // RWKV_HEAD
DECAY_LORA = 64
ICLR_LORA = 64
VALUE_LORA = 32
GATE_LORA = 128
RWKV_GN_EPS = 64e-5
MAMBA_EXPAND = 2
MAMBA_INNER = MAMBA_EXPAND * D_MODEL
MAMBA_HEAD = 64
MAMBA_HEADS = MAMBA_INNER // MAMBA_HEAD
MAMBA_GROUPS = 4
MAMBA_STATE = 128
MAMBA_CONV = 4
SSD_CHUNK = 128
MAMBA_XBC = MAMBA_INNER + 2 * MAMBA_GROUPS * MAMBA_STATE
RWKV_COLS = 3 * RWKV_WIDTH + DECAY_LORA + ICLR_LORA + GATE_LORA
MAMBA_COLS = MAMBA_INNER + MAMBA_XBC + MAMBA_HEADS
GATE_COLS = 2 * D_MODEL
IN_COLS = RWKV_COLS + MAMBA_COLS + GATE_COLS
N_EXPERT_GROUPS = 4
EXPERTS_PER_GROUP = 8
N_EXPERTS = N_EXPERT_GROUPS * EXPERTS_PER_GROUP
TOP_K_INNER = 2
EXPERT_FF = D_MODEL // 2
EXPERT_BLOCK = 256
NORM_EPS = 1e-6

kernel_name = 'hybrid_rwkv7_mamba2_hmoe_block'


def _rmsnorm(x, g):
    xf = x.astype(jnp.float32)
    y = xf * lax.rsqrt(jnp.mean(xf * xf, axis=-1, keepdims=True) + NORM_EPS)
    return (y * g.astype(jnp.float32)).astype(x.dtype)


def _token_shift(u):
    return jnp.pad(u[:, :-1], ((0, 0), (1, 0), (0, 0)))


def _wkv7(r, decay, k, v, a_vec, b_vec):
    bsz, _, nh, n = r.shape

    def step(state, inp):
        r_t, w_t, k_t, v_t, a_t, b_t = inp
        sa = jnp.einsum('bhvk,bhk->bhv', state, a_t)
        state = (state * w_t[:, :, None, :] + sa[..., None] * b_t[:, :, None, :]
                 + v_t[..., None] * k_t[:, :, None, :])
        return state, jnp.einsum('bhvk,bhk->bhv', state, r_t)

    xs = tuple(jnp.swapaxes(t, 0, 1) for t in (r, decay, k, v, a_vec, b_vec))
    state0 = jnp.zeros((bsz, nh, n, n), jnp.float32)
    _, y = lax.scan(step, state0, xs)
    return jnp.swapaxes(y, 0, 1)


def _rwkv7_branch(u, h, v_first, v_res, mu, w0, w_decay2, a0, w_iclr2, w_gate2,
                  k_k, k_a, r_k, lnx_g, lnx_b, w_o):
    bsz, s, _ = u.shape
    u = u.astype(jnp.float32)
    u = u + (_token_shift(u) - u) * mu
    splits = [RWKV_WIDTH, 2 * RWKV_WIDTH, 3 * RWKV_WIDTH,
              3 * RWKV_WIDTH + DECAY_LORA, 3 * RWKV_WIDTH + DECAY_LORA + ICLR_LORA]
    r, k, v, w_lo, a_lo, g_lo = jnp.split(u, splits, axis=-1)
    w = -jax.nn.softplus(-(w0 + jnp.tanh(w_lo) @ w_decay2)) - 0.5
    decay = jnp.exp(-jnp.exp(w))
    a = jax.nn.sigmoid(a0 + a_lo @ w_iclr2)
    g = jax.nn.sigmoid(g_lo) @ w_gate2
    if v_res is not None:
        w_v1, mu_v, v0, w_v2 = v_res
        v_lo = (h @ w_v1).astype(jnp.float32)
        v_lo = v_lo + (_token_shift(v_lo) - v_lo) * mu_v
        v = v + (v_first - v) * jax.nn.sigmoid(v0 + v_lo @ w_v2)
    heads = lambda t: t.reshape(bsz, s, RWKV_HEADS, RWKV_HEAD)
    kk = heads(k * k_k)
    kk = kk / jnp.maximum(jnp.linalg.norm(kk, axis=-1, keepdims=True), 1e-12)
    k = k * (1.0 + (a - 1.0) * k_a)
    r_h, k_h, v_h, a_h, d_h = heads(r), heads(k), heads(v), heads(a), heads(decay)
    y = _wkv7(r_h, d_h, k_h, v_h, -kk, kk * a_h)
    mean = jnp.mean(y, axis=-1, keepdims=True)
    var = jnp.mean(jnp.square(y - mean), axis=-1, keepdims=True)
    y = ((y - mean) * lax.rsqrt(var + RWKV_GN_EPS)).reshape(bsz, s, RWKV_WIDTH) * lnx_g + lnx_b
    bonus = jnp.sum(r_h * k_h * r_k, axis=-1, keepdims=True) * v_h
    y = y + bonus.reshape(bsz, s, RWKV_WIDTH)
    return (y * g) @ w_o, v


def _causal_dwconv(u, w, b):
    kw, c = w.shape
    out = lax.conv_general_dilated(u, w[:, None, :].astype(u.dtype), window_strides=(1,),
                                   padding=[(kw - 1, 0)], dimension_numbers=('NWC', 'WIO', 'NWC'),
                                   feature_group_count=c)
    return out + b


def _ssd(xh, dt, a_head, bm, cm):
    b, s, nh, p = xh.shape
    g, n = bm.shape[2], bm.shape[3]
    hg = nh // g
    nc = s // SSD_CHUNK
    ln = SSD_CHUNK
    xdt = jnp.moveaxis((xh * dt[..., None]).reshape(b, nc, ln, g, hg, p), 1, 0)
    adt = jnp.moveaxis((dt * a_head).reshape(b, nc, ln, g, hg), 1, 0)
    bc = jnp.moveaxis(bm.reshape(b, nc, ln, g, n), 1, 0)
    cc = jnp.moveaxis(cm.reshape(b, nc, ln, g, n), 1, 0)
    causal = jnp.tril(jnp.ones((ln, ln), bool))[None, :, :, None, None]

    def chunk_step(state, inp):
        x_c, a_c, b_c, c_c = inp
        a_cs = jnp.cumsum(a_c, axis=1)
        seg = a_cs[:, :, None] - a_cs[:, None, :]
        decay_ls = jnp.exp(jnp.where(causal, seg, -jnp.inf))
        cb = jnp.einsum('blgn,bsgn->blsg', c_c, b_c)
        y_diag = jnp.einsum('blsg,blsgj,bsgjp->blgjp', cb, decay_ls, x_c)
        y_off = jnp.einsum('blgn,bgjpn,blgj->blgjp', c_c, state, jnp.exp(a_cs))
        a_last = a_cs[:, -1]
        to_end = jnp.exp(a_last[:, None] - a_cs)
        new_state = (state * jnp.exp(a_last)[..., None, None]
                     + jnp.einsum('blgn,blgj,blgjp->bgjpn', b_c, to_end, x_c))
        return new_state, y_diag + y_off

    state0 = jnp.zeros((b, g, hg, p, n), jnp.float32)
    _, y = lax.scan(chunk_step, state0, (xdt, adt, bc, cc))
    return jnp.moveaxis(y, 0, 1).reshape(b, s, nh, p)


def _mamba2_branch(m, conv_w, conv_b, dt_bias, a_log, d_skip, mnorm_g, w_o):
    bsz, s, _ = m.shape
    z, xbc, dt_raw = jnp.split(m.astype(jnp.float32), [MAMBA_INNER, MAMBA_INNER + MAMBA_XBC], axis=-1)
    xbc = jax.nn.silu(_causal_dwconv(xbc, conv_w.astype(jnp.float32), conv_b))
    xs, bm, cm = jnp.split(xbc, [MAMBA_INNER, MAMBA_INNER + MAMBA_GROUPS * MAMBA_STATE], axis=-1)
    xh = xs.reshape(bsz, s, MAMBA_HEADS, MAMBA_HEAD)
    bm = bm.reshape(bsz, s, MAMBA_GROUPS, MAMBA_STATE)
    cm = cm.reshape(bsz, s, MAMBA_GROUPS, MAMBA_STATE)
    dt = jax.nn.softplus(dt_raw + dt_bias)
    a_head = -jnp.exp(a_log.astype(jnp.float32))
    y = _ssd(xh, dt, a_head, bm, cm) + xh * d_skip[:, None]
    y = y.reshape(bsz, s, MAMBA_INNER) * jax.nn.silu(z)
    yg = y.reshape(bsz, s, MAMBA_GROUPS, MAMBA_INNER // MAMBA_GROUPS)
    yg = yg * lax.rsqrt(jnp.mean(yg * yg, axis=-1, keepdims=True) + 1e-5)
    y = yg.reshape(bsz, s, MAMBA_INNER) * mnorm_g
    return y @ w_o


def _hier_moe(h, w_rg, b_rg, w_re, b_re, w_e1, w_e3, w_e2):
    bsz, s, d = h.shape
    t = bsz * s
    hf = h.reshape(t, d)
    g_prob = jax.nn.softmax((hf @ w_rg).astype(jnp.float32) + b_rg, axis=-1)
    g_p, g_idx = lax.top_k(g_prob, 1)
    e_logits = ((hf @ w_re).astype(jnp.float32) + b_re).reshape(t, N_EXPERT_GROUPS, EXPERTS_PER_GROUP)
    e_sel = jnp.take_along_axis(e_logits, g_idx[:, :, None], axis=1)[:, 0]
    e_p, e_idx = lax.top_k(jax.nn.softmax(e_sel, axis=-1), TOP_K_INNER)
    e_p = e_p / jnp.sum(e_p, axis=-1, keepdims=True)
    weights = (g_p * e_p).reshape(-1)
    eid = (g_idx * EXPERTS_PER_GROUP + e_idx).reshape(-1).astype(jnp.int32)
    tok = jnp.repeat(jnp.arange(t, dtype=jnp.int32), TOP_K_INNER)
    n_assign = t * TOP_K_INNER
    order = jnp.argsort(eid)
    se, stok, sw = eid[order], tok[order], weights[order]
    counts = jnp.bincount(eid, length=N_EXPERTS)
    offsets = jnp.cumsum(counts) - counts
    pcounts = (counts + EXPERT_BLOCK - 1) // EXPERT_BLOCK * EXPERT_BLOCK
    pends = jnp.cumsum(pcounts)
    poffsets = pends - pcounts
    pos = poffsets[se] + (jnp.arange(n_assign, dtype=jnp.int32) - offsets[se])
    n_blocks = -(-(n_assign + N_EXPERTS * (EXPERT_BLOCK - 1)) // EXPERT_BLOCK)
    n_rows = n_blocks * EXPERT_BLOCK
    buf_tok = jnp.zeros((n_rows,), jnp.int32).at[pos].set(stok)
    buf_w = jnp.zeros((n_rows,), jnp.float32).at[pos].set(sw)
    block_start = jnp.arange(n_blocks, dtype=jnp.int32) * EXPERT_BLOCK
    block_e = jnp.minimum(jnp.searchsorted(pends, block_start, side='right'), N_EXPERTS - 1)
    xb = hf[buf_tok].reshape(n_blocks, EXPERT_BLOCK, d)

    def expert_block(args):
        xblk, e = args
        hid = jax.nn.silu(xblk @ w_e1[e]) * (xblk @ w_e3[e])
        return hid @ w_e2[e]

    yb = lax.map(expert_block, (xb, block_e)).reshape(n_rows, d)
    y = jax.ops.segment_sum(yb * buf_w[:, None].astype(yb.dtype), buf_tok, num_segments=t)
    return y.reshape(bsz, s, d).astype(h.dtype)


def setup_inputs(seed: int = 0) -> dict:
    key = jax.random.key(seed)
    ks = jax.random.split(key, 40)
    f32 = jnp.float32
    nrm = lambda k, shape, scale: jax.random.normal(k, shape, f32) * scale
    uni = lambda k, shape, lo, hi: jax.random.uniform(k, shape, f32, minval=lo, maxval=hi)
    nv = DEPTH - 1
    dt = jnp.exp(uni(ks[21], (DEPTH, MAMBA_HEADS), float(np.log(1e-3)), float(np.log(1e-1))))
    return {
        'x': nrm(ks[0], (BATCH, SEQ, D_MODEL), 1.0),
        'norm1_g': 1.0 + nrm(ks[1], (DEPTH, D_MODEL), 0.02),
        'w_in': nrm(ks[2], (DEPTH, D_MODEL, IN_COLS), D_MODEL ** -0.5),
        'mu_rwkv': uni(ks[3], (DEPTH, RWKV_COLS), 0.0, 1.0),
        'w0': uni(ks[4], (DEPTH, RWKV_WIDTH), -5.0, 1.0),
        'w_decay2': nrm(ks[5], (DEPTH, DECAY_LORA, RWKV_WIDTH), 0.1 * DECAY_LORA ** -0.5),
        'a0': nrm(ks[6], (DEPTH, RWKV_WIDTH), 0.1),
        'w_iclr2': nrm(ks[7], (DEPTH, ICLR_LORA, RWKV_WIDTH), 0.1 * ICLR_LORA ** -0.5),
        'w_gate2': nrm(ks[8], (DEPTH, GATE_LORA, RWKV_WIDTH), GATE_LORA ** -0.5),
        'k_k': 0.85 + nrm(ks[9], (DEPTH, RWKV_WIDTH), 0.02),
        'k_a': 1.0 + nrm(ks[10], (DEPTH, RWKV_WIDTH), 0.02),
        'r_k': nrm(ks[11], (DEPTH, RWKV_HEADS, RWKV_HEAD), 0.1),
        'lnx_g': 1.0 + nrm(ks[12], (DEPTH, RWKV_WIDTH), 0.02),
        'lnx_b': nrm(ks[13], (DEPTH, RWKV_WIDTH), 0.01),
        'w_rwkv_o': nrm(ks[14], (DEPTH, RWKV_WIDTH, D_MODEL), RWKV_WIDTH ** -0.5),
        'w_v1': nrm(ks[15], (nv, D_MODEL, VALUE_LORA), D_MODEL ** -0.5),
        'mu_v': uni(ks[16], (nv, VALUE_LORA), 0.0, 1.0),
        'v0': nrm(ks[17], (nv, RWKV_WIDTH), 0.1),
        'w_v2': nrm(ks[18], (nv, VALUE_LORA, RWKV_WIDTH), 0.1 * VALUE_LORA ** -0.5),
        'conv_w': nrm(ks[19], (DEPTH, MAMBA_CONV, MAMBA_XBC), MAMBA_CONV ** -0.5),
        'conv_b': nrm(ks[20], (DEPTH, MAMBA_XBC), 0.01),
        'dt_bias': dt + jnp.log(-jnp.expm1(-dt)),
        'a_log': jnp.log(uni(ks[22], (DEPTH, MAMBA_HEADS), 1.0, 16.0)),
        'd_skip': 1.0 + nrm(ks[23], (DEPTH, MAMBA_HEADS), 0.02),
        'mnorm_g': 1.0 + nrm(ks[24], (DEPTH, MAMBA_INNER), 0.02),
        'w_mamba_o': nrm(ks[25], (DEPTH, MAMBA_INNER, D_MODEL), MAMBA_INNER ** -0.5),
        'w_out': nrm(ks[26], (DEPTH, D_MODEL, D_MODEL), D_MODEL ** -0.5),
        'norm2_g': 1.0 + nrm(ks[27], (DEPTH, D_MODEL), 0.02),
        'w_rg': nrm(ks[28], (DEPTH, D_MODEL, N_EXPERT_GROUPS), D_MODEL ** -0.5),
        'b_rg': nrm(ks[29], (DEPTH, N_EXPERT_GROUPS), 0.01),
        'w_re': nrm(ks[30], (DEPTH, D_MODEL, N_EXPERTS), D_MODEL ** -0.5),
        'b_re': nrm(ks[31], (DEPTH, N_EXPERTS), 0.01),
        'w_e1': nrm(ks[32], (DEPTH, N_EXPERTS, D_MODEL, EXPERT_FF), D_MODEL ** -0.5),
        'w_e3': nrm(ks[33], (DEPTH, N_EXPERTS, D_MODEL, EXPERT_FF), D_MODEL ** -0.5),
        'w_e2': nrm(ks[34], (DEPTH, N_EXPERTS, EXPERT_FF, D_MODEL), EXPERT_FF ** -0.5),
        'final_g': 1.0 + nrm(ks[35], (D_MODEL,), 0.02),
    }


def reference(x, norm1_g, w_in, mu_rwkv, w0, w_decay2, a0, w_iclr2, w_gate2, k_k, k_a, r_k,
              lnx_g, lnx_b, w_rwkv_o, w_v1, mu_v, v0, w_v2, conv_w, conv_b, dt_bias, a_log,
              d_skip, mnorm_g, w_mamba_o, w_out, norm2_g, w_rg, b_rg, w_re, b_re, w_e1, w_e3,
              w_e2, final_g):
    v_first = None
    for l in range(DEPTH):
        h = _rmsnorm(x, norm1_g[l])
        proj = h @ w_in[l]
        u_rwkv, u_mamba, u_gate = jnp.split(proj, [RWKV_COLS, RWKV_COLS + MAMBA_COLS], axis=-1)
        v_res = None if l == 0 else (w_v1[l - 1], mu_v[l - 1], v0[l - 1], w_v2[l - 1])
        o_a, v_l = _rwkv7_branch(u_rwkv, h, v_first, v_res, mu_rwkv[l], w0[l], w_decay2[l], a0[l],
                                 w_iclr2[l], w_gate2[l], k_k[l], k_a[l], r_k[l], lnx_g[l],
                                 lnx_b[l], w_rwkv_o[l])
        if l == 0:
            v_first = v_l
        o_b = _mamba2_branch(u_mamba, conv_w[l], conv_b[l], dt_bias[l], a_log[l], d_skip[l],
                             mnorm_g[l], w_mamba_o[l])
        g_a, g_b = jnp.split(jax.nn.sigmoid(u_gate.astype(jnp.float32)), 2, axis=-1)
        merged = (g_a * o_a + g_b * o_b).astype(x.dtype)
        x = x + (merged @ w_out[l]).astype(x.dtype)
        h2 = _rmsnorm(x, norm2_g[l])
        x = x + _hier_moe(h2, w_rg[l], b_rg[l], w_re[l], b_re[l], w_e1[l], w_e3[l], w_e2[l])
    return _rmsnorm(x, final_g)
```

```python
import functools

import jax
import jax.numpy as jnp
from jax import lax
from jax.experimental import pallas as pl
from jax.experimental.pallas import tpu as pltpu

F32 = jnp.float32
BF16 = jnp.bfloat16

D_MODEL = 1024
RWKV_HEAD = 64
DECAY_LORA = 64
ICLR_LORA = 64
VALUE_LORA = 32
GATE_LORA = 128
RWKV_GN_EPS = 64e-5
MAMBA_INNER = 2048
MAMBA_HEAD = 64
MAMBA_HEADS = 32
MAMBA_GROUPS = 4
MAMBA_STATE = 128
MAMBA_CONV = 4
SSD_CHUNK = 128
N_EXPERT_GROUPS = 4
EXPERTS_PER_GROUP = 8
N_EXPERTS = 32
EXPERT_FF = 512
NORM_EPS = 1e-6
MAMBA_NORM_EPS = 1e-5

LANES = 128
SUBLANES = 8
D_TILES = D_MODEL // LANES
N_PAIRS = D_MODEL // LANES
WKV_CHUNK = 64
EXPERT_ROWS = 256

COL_Z = 0
COL_GATE = 2048
COL_XS = 4096
COL_R = 6144
COL_K = 7168
COL_V = 8192
COL_BC = 9216
COL_WA = 10240
COL_GLO = 10368
COL_DTV = 10496
N_COLS = 10752
VLO_LANE = 32

VMEM_LIMIT = 56 * 1024 * 1024


def _softplus(x):
    return jnp.maximum(x, 0.0) + jnp.log(1.0 + jnp.exp(-jnp.abs(x)))


def _sigmoid(x):
    return 1.0 / (1.0 + jnp.exp(-x))


def _dot(a, b):
    return jnp.dot(a, b, preferred_element_type=F32)


def _dot_nt(a, b):
    return lax.dot_general(a, b, (((1,), (1,)), ((), ())), preferred_element_type=F32)


def _split_hi_lo(x):
    hi = x.astype(BF16)
    lo = (x - hi.astype(F32)).astype(BF16)
    return hi, lo


def _inproj_kernel(x_ref, g_ref, w_ref, o_ref, h_scr):
    @pl.when(pl.program_id(1) == 0)
    def _():
        x = x_ref[...]
        ms = jnp.mean(x * x, axis=-1, keepdims=True)
        h_scr[...] = (x * lax.rsqrt(ms + NORM_EPS) * g_ref[...]).astype(BF16)

    o_ref[...] = _dot(h_scr[...], w_ref[...]).astype(o_ref.dtype)


def _inproj(x2d, g_row, w_bf16, tm, tn):
    t, d = x2d.shape
    nc = w_bf16.shape[1]
    return pl.pallas_call(
        _inproj_kernel,
        out_shape=jax.ShapeDtypeStruct((t, nc), BF16),
        grid=(t // tm, nc // tn),
        in_specs=[
            pl.BlockSpec((tm, d), lambda i, j: (i, 0)),
            pl.BlockSpec((1, d), lambda i, j: (0, 0)),
            pl.BlockSpec((d, tn), lambda i, j: (0, j)),
        ],
        out_specs=pl.BlockSpec((tm, tn), lambda i, j: (i, j)),
        scratch_shapes=[pltpu.VMEM((tm, d), BF16)],
        compiler_params=pltpu.CompilerParams(
            dimension_semantics=("arbitrary", "arbitrary"), vmem_limit_bytes=VMEM_LIMIT),
        name="inproj",
    )(x2d, g_row, w_bf16)


RP_MU_R, RP_MU_K, RP_MU_V, RP_W0, RP_A0, RP_V0, RP_KK, RP_KA, RP_RK, RP_LNG, RP_LNB = range(11)
RP_ROWS = 16
SP_MU_WA, SP_MU_G, SP_MU_VL = range(3)
LR_DECAY, LR_ICLR, LR_GATE, LR_VAL = range(4)


def _bd_stack(x, m0):
    zero = jnp.zeros_like(x)
    return jnp.concatenate([jnp.where(m0, x, zero), jnp.where(m0, zero, x)], axis=0)


def _rwkv_kernel(has_vres, *refs):
    if has_vres:
        (r_ref, k_ref, v_ref, wa_ref, gl_ref, dtv_ref, vf_ref, rowp_ref, smallp_ref, lora_ref,
         y_ref, c_rkv, c_small, state, s_r, s_k, s_v, s_vf, s_small, s_y) = refs
        vout_ref = None
    else:
        (r_ref, k_ref, v_ref, wa_ref, gl_ref, rowp_ref, smallp_ref, lora_ref,
         y_ref, vout_ref, c_rkv, c_small, state, s_r, s_k, s_v, s_small, s_y) = refs
        dtv_ref = vf_ref = s_vf = None
    L = WKV_CHUNK
    c = pl.program_id(1)
    first = c == 0
    row = lax.broadcasted_iota(jnp.int32, (L, LANES), 0)
    lane = lax.broadcasted_iota(jnp.int32, (L, LANES), 1)
    row0 = row == 0

    @pl.when(first)
    def _():
        state[...] = jnp.zeros_like(state)
        c_rkv[...] = jnp.zeros_like(c_rkv)
        c_small[...] = jnp.zeros_like(c_small)

    def shift_mix(u, prev_row, mu):
        prev = jnp.where(row0, prev_row, pltpu.roll(u, 1, 0))
        return u + (prev - u) * mu

    for p in range(N_PAIRS):
        sl = slice(p * LANES, (p + 1) * LANES)
        rp = rowp_ref[p]
        for idx, (src, dst, mu_slot) in enumerate(((r_ref, s_r, RP_MU_R), (k_ref, s_k, RP_MU_K),
                                                   (v_ref, s_v, RP_MU_V))):
            u = src[0, :, sl].astype(F32)
            prev_row = c_rkv[idx, 0:1, sl]
            mixed = shift_mix(u, prev_row, rp[mu_slot:mu_slot + 1])
            c_rkv[idx, 0:1, sl] = u[L - 1:L]
            dst[p] = mixed
            if idx == 2 and vout_ref is not None:
                vout_ref[0, :, sl] = mixed.astype(vout_ref.dtype)
        if has_vres:
            s_vf[p] = vf_ref[0, :, sl].astype(F32)
    small_srcs = [(wa_ref, SP_MU_WA), (gl_ref, SP_MU_G)]
    if has_vres:
        small_srcs.append((dtv_ref, SP_MU_VL))
    for idx, (src, mu_slot) in enumerate(small_srcs):
        u = src[0].astype(F32)
        prev_row = c_small[idx, 0:1]
        s_small[idx] = shift_mix(u, prev_row, smallp_ref[mu_slot:mu_slot + 1])
        c_small[idx, 0:1] = u[L - 1:L]

    xwa = s_small[0]
    tanh_wa = jnp.tanh(xwa).astype(BF16)
    xwa_bf = xwa.astype(BF16)
    sig_g = _sigmoid(s_small[1]).astype(BF16)
    xvl_bf = s_small[2].astype(BF16) if has_vres else None

    m0 = lane < RWKV_HEAD
    s_in = lane % RWKV_HEAD
    strict = s_in < row
    incl = s_in <= row
    eye_w = (s_in == row).astype(F32)
    lane2 = lax.broadcasted_iota(jnp.int32, (2 * L, LANES), 1)
    row2 = lax.broadcasted_iota(jnp.int32, (2 * L, LANES), 0)
    bd_mask = (lane2 // RWKV_HEAD) == (row2 // RWKV_HEAD)
    bd_ones = bd_mask.astype(BF16)
    tril = (lax.broadcasted_iota(jnp.int32, (L, L), 0)
            >= lax.broadcasted_iota(jnp.int32, (L, L), 1)).astype(BF16)

    def pair_body(p, carry):
        rp = rowp_ref[p]
        r = s_r[p]
        xk = s_k[p]
        v = s_v[p]
        w_in = rp[RP_W0:RP_W0 + 1] + _dot(tanh_wa, lora_ref[LR_DECAY, p])
        lw = -jnp.exp(-_softplus(-w_in) - 0.5)
        a = _sigmoid(rp[RP_A0:RP_A0 + 1] + _dot(xwa_bf, lora_ref[LR_ICLR, p]))
        g = _dot(sig_g, lora_ref[LR_GATE, p])
        if has_vres:
            mix = _sigmoid(rp[RP_V0:RP_V0 + 1] + _dot(xvl_bf, lora_ref[LR_VAL, p]))
            v = v + (s_vf[p] - v) * mix
        kkr = xk * rp[RP_KK:RP_KK + 1]
        ss = _dot((kkr * kkr).astype(BF16), bd_ones)
        kkn = kkr / jnp.maximum(jnp.sqrt(ss), 1e-12)
        k2 = xk * (1.0 + (a - 1.0) * rp[RP_KA:RP_KA + 1])
        avec = -kkn
        bvec = kkn * a

        lw_hi, lw_lo = _split_hi_lo(lw)
        w_cum = _dot(tril, lw_hi) + _dot(tril, lw_lo)
        w_mid = w_cum[L // 2 - 1:L // 2]
        w_end = w_cum[L - 1:L]
        e_abs = jnp.exp(w_cum)
        e_prev = jnp.exp(w_cum - lw)
        e_from_mid = jnp.exp(w_mid - w_cum)
        e_to_end = jnp.exp(w_end - w_cum)
        e_mid_inv = jnp.exp(-w_mid)
        r_abs = r * e_abs
        a_abs = avec * e_prev
        r_mid = r_abs * e_mid_inv
        a_mid = a_abs * e_mid_inv
        b_mid = bvec * e_from_mid
        k_mid = k2 * e_from_mid
        b_end = bvec * e_to_end
        k_end = k2 * e_to_end

        st = state[p]
        lhs_abs = jnp.concatenate([a_abs, r_abs], axis=0).astype(BF16)
        m1 = _dot_nt(lhs_abs, st.astype(BF16))
        lhs_mid = jnp.concatenate([a_mid, r_mid], axis=0).astype(BF16)
        rhs_mid = jnp.concatenate([_bd_stack(b_mid, m0), _bd_stack(k_mid, m0)], axis=0).astype(BF16)
        m2 = _dot_nt(lhs_mid, rhs_mid)
        a_ab = jnp.where(strict, m2[0:L, 0:LANES], 0.0)
        a_ak = jnp.where(strict, m2[0:L, LANES:2 * LANES], 0.0)
        a_rb = jnp.where(incl, m2[L:2 * L, 0:LANES], 0.0)
        a_rk = jnp.where(incl, m2[L:2 * L, LANES:2 * LANES], 0.0)

        v_bd = _bd_stack(v, m0).astype(BF16)
        rhs_u = m1[0:L] + _dot(a_ak.astype(BF16), v_bd)

        x_inv = eye_w + a_ab
        pw = _dot(a_ab.astype(BF16), _bd_stack(a_ab, m0).astype(BF16))
        for _ in range(4):
            res = _dot(jnp.concatenate([x_inv, pw], axis=0).astype(BF16), _bd_stack(pw, m0).astype(BF16))
            x_inv = x_inv + res[0:L]
            pw = res[L:2 * L]
        x_inv = x_inv + _dot(x_inv.astype(BF16), _bd_stack(pw, m0).astype(BF16))

        u = _dot(x_inv.astype(BF16), _bd_stack(rhs_u, m0).astype(BF16))
        u_bd = _bd_stack(u, m0).astype(BF16)
        y = m1[L:2 * L] + _dot(jnp.concatenate([a_rb, a_rk], axis=1).astype(BF16),
                               jnp.concatenate([u_bd, v_bd], axis=0))

        uv_t = jnp.concatenate([u, v], axis=0).T.astype(BF16)
        upd = _dot(uv_t, jnp.concatenate([b_end, k_end], axis=0).astype(BF16))
        state[p] = st * jnp.exp(w_end) + jnp.where(bd_mask, upd, 0.0)

        inv_n = 1.0 / RWKV_HEAD
        mean = _dot(y.astype(BF16), bd_ones) * inv_n
        yc = y - mean
        var = _dot((yc * yc).astype(BF16), bd_ones) * inv_n
        yn = yc * lax.rsqrt(var + RWKV_GN_EPS) * rp[RP_LNG:RP_LNG + 1] + rp[RP_LNB:RP_LNB + 1]
        bonus = _dot((r * k2 * rp[RP_RK:RP_RK + 1]).astype(BF16), bd_ones) * v
        s_y[p] = (yn + bonus) * g
        return carry

    lax.fori_loop(0, N_PAIRS, pair_body, 0)
    for p in range(N_PAIRS):
        y_ref[0, :, p * LANES:(p + 1) * LANES] = s_y[p].astype(y_ref.dtype)


def _rwkv_branch(proj3, v_first, rowp, smallp, lora):
    b, s, _ = proj3.shape
    L = WKV_CHUNK
    has_vres = v_first is not None
    wide = lambda col: pl.BlockSpec((1, L, D_MODEL), lambda i, j, c=col // D_MODEL: (i, j, c))
    narrow = lambda col: pl.BlockSpec((1, L, LANES), lambda i, j, c=col // LANES: (i, j, c))
    full = lambda shape: pl.BlockSpec(shape, lambda i, j: (0,) * len(shape))
    out_blk = pl.BlockSpec((1, L, D_MODEL), lambda i, j: (i, j, 0))
    in_specs = [wide(COL_R), wide(COL_K), wide(COL_V), narrow(COL_WA), narrow(COL_GLO)]
    args = [proj3] * 5
    if has_vres:
        in_specs += [narrow(COL_DTV), out_blk]
        args += [proj3, v_first]
    in_specs += [full(rowp.shape), full(smallp.shape), full(lora.shape)]
    args += [rowp, smallp, lora]
    pm = lambda: pltpu.VMEM((N_PAIRS, L, LANES), F32)
    scratch = [pltpu.VMEM((3, SUBLANES, D_MODEL), F32), pltpu.VMEM((3, SUBLANES, LANES), F32),
               pltpu.VMEM((N_PAIRS, 2 * RWKV_HEAD, LANES), F32), pm(), pm(), pm()]
    if has_vres:
        scratch.append(pm())
    scratch += [pltpu.VMEM((3, L, LANES), F32), pm()]
    y_shape = jax.ShapeDtypeStruct((b, s, D_MODEL), BF16)
    out_shape = y_shape if has_vres else (y_shape, y_shape)
    out_specs = out_blk if has_vres else (out_blk, out_blk)
    return pl.pallas_call(
        functools.partial(_rwkv_kernel, has_vres),
        out_shape=out_shape,
        grid=(b, s // L),
        in_specs=in_specs,
        out_specs=out_specs,
        scratch_shapes=scratch,
        compiler_params=pltpu.CompilerParams(
            dimension_semantics=("arbitrary", "arbitrary"), vmem_limit_bytes=VMEM_LIMIT),
        name="rwkv7",
    )(*args)


HEADS_PER_GROUP = MAMBA_HEADS // MAMBA_GROUPS
GROUP_WIDTH = MAMBA_INNER // MAMBA_GROUPS
BC_WIDTH = 2 * MAMBA_GROUPS * MAMBA_STATE
DP_DT_BIAS, DP_A_HEAD = range(2)


def _silu(x):
    return x * _sigmoid(x)


def _mamba_kernel(z_ref, xs_ref, bc_ref, dtv_ref, cw_xs_ref, cb_xs_ref, cw_bc_ref, cb_bc_ref, dtp_ref,
                  dskip_ref, mnorm_ref, expand_ref, y_ref, tail_xs, tail_bc, state, s_xc, s_y):
    L = SSD_CHUNK
    c = pl.program_id(1)
    first = c == 0

    @pl.when(first)
    def _():
        state[...] = jnp.zeros_like(state)
        tail_xs[...] = jnp.zeros_like(tail_xs)
        tail_bc[...] = jnp.zeros_like(tail_bc)

    def conv_silu(u, tail_ref, w_ref, b_ref):
        width = u.shape[1]
        row8 = lax.broadcasted_iota(jnp.int32, (SUBLANES, width), 0)
        tail = tail_ref[...]
        acc = u * w_ref[MAMBA_CONV - 1:MAMBA_CONV] + b_ref[...]
        for d in range(1, MAMBA_CONV):
            rolled = pltpu.roll(u, d, 0)
            head = jnp.where(row8 < d, pltpu.roll(tail, d, 0), rolled[0:SUBLANES])
            shifted = jnp.concatenate([head, rolled[SUBLANES:]], axis=0)
            acc = acc + shifted * w_ref[MAMBA_CONV - 1 - d:MAMBA_CONV - d]
        tail_ref[...] = u[L - SUBLANES:L]
        return _silu(acc)

    xc = conv_silu(xs_ref[0].astype(F32), tail_xs, cw_xs_ref, cb_xs_ref)
    bcc = conv_silu(bc_ref[0].astype(F32), tail_bc, cw_bc_ref, cb_bc_ref)
    s_xc[...] = xc

    dt = _softplus(dtv_ref[0].astype(F32) + dtp_ref[DP_DT_BIAS:DP_DT_BIAS + 1])
    adt = dt * dtp_ref[DP_A_HEAD:DP_A_HEAD + 1]
    tril_b = lax.broadcasted_iota(jnp.int32, (L, L), 0) >= lax.broadcasted_iota(jnp.int32, (L, L), 1)
    tril = tril_b.astype(BF16)
    adt_hi, adt_lo = _split_hi_lo(adt)
    a_cs = _dot(tril, adt_hi) + _dot(tril, adt_lo)
    a_cs_t = a_cs.T
    exp_acs = jnp.exp(a_cs)
    to_end = jnp.exp(a_cs[L - 1:L] - a_cs)
    expand = expand_ref[...]
    dt_x = _dot(dt.astype(BF16), expand)
    exp_acs_x = _dot(exp_acs.astype(BF16), expand)
    to_end_x = _dot(to_end.astype(BF16), expand)
    xdt = xc * dt_x
    xdt_bf = xdt.astype(BF16)
    xw_bf = (xdt * to_end_x).astype(BF16)
    lane = lax.broadcasted_iota(jnp.int32, (L, LANES), 1)
    m0 = lane < MAMBA_HEAD

    for g in range(MAMBA_GROUPS):
        bg = bcc[:, g * MAMBA_STATE:(g + 1) * MAMBA_STATE]
        cg = bcc[:, (MAMBA_GROUPS + g) * MAMBA_STATE:(MAMBA_GROUPS + g + 1) * MAMBA_STATE]
        bg_bf = bg.astype(BF16)
        cg_bf = cg.astype(BF16)
        cb = _dot_nt(cg_bf, bg_bf)
        gs = slice(g * GROUP_WIDTH, (g + 1) * GROUP_WIDTH)
        st = state[g]
        y_off = _dot(cg_bf, st.astype(BF16)) * exp_acs_x[:, gs]
        for jp in range(HEADS_PER_GROUP // 2):
            ws = []
            for hh in range(2):
                j = g * HEADS_PER_GROUP + 2 * jp + hh
                seg = a_cs[:, j:j + 1] - a_cs_t[j:j + 1, :]
                dec = jnp.where(tril_b, jnp.exp(jnp.minimum(seg, 0.0)), 0.0)
                ws.append((cb * dec).astype(BF16))
            ps = slice(g * GROUP_WIDTH + jp * LANES, g * GROUP_WIDTH + (jp + 1) * LANES)
            y_diag = _dot(jnp.concatenate(ws, axis=1), _bd_stack(xdt_bf[:, ps], m0))
            s_y[:, ps] = y_diag + y_off[:, jp * LANES:(jp + 1) * LANES]
        state[g] = st * exp_acs_x[L - 1:L, gs] + _dot(bg.T.astype(BF16), xw_bf[:, gs])

    y = s_y[...] + s_xc[...] * dskip_ref[...]
    y = y * _silu(z_ref[0].astype(F32))
    for g in range(MAMBA_GROUPS):
        gs = slice(g * GROUP_WIDTH, (g + 1) * GROUP_WIDTH)
        yg = y[:, gs]
        ms = jnp.mean(yg * yg, axis=-1, keepdims=True)
        y_ref[0, :, gs] = (yg * lax.rsqrt(ms + MAMBA_NORM_EPS) * mnorm_ref[:, gs]).astype(y_ref.dtype)


def _mamba_branch(proj3, cw_xs, cb_xs, cw_bc, cb_bc, dtp, dskip_x, mnorm, expand):
    b, s, _ = proj3.shape
    L = SSD_CHUNK
    blk = lambda width, col: pl.BlockSpec((1, L, width), lambda i, j, c=col // width: (i, j, c))
    full = lambda arr: pl.BlockSpec(arr.shape, lambda i, j: (0,) * arr.ndim)
    params = [cw_xs, cb_xs, cw_bc, cb_bc, dtp, dskip_x, mnorm, expand]
    return pl.pallas_call(
        _mamba_kernel,
        out_shape=jax.ShapeDtypeStruct((b, s, MAMBA_INNER), BF16),
        grid=(b, s // L),
        in_specs=[blk(MAMBA_INNER, COL_Z), blk(MAMBA_INNER, COL_XS), blk(BC_WIDTH, COL_BC),
                  blk(LANES, COL_DTV)] + [full(a) for a in params],
        out_specs=pl.BlockSpec((1, L, MAMBA_INNER), lambda i, j: (i, j, 0)),
        scratch_shapes=[pltpu.VMEM((SUBLANES, MAMBA_INNER), F32), pltpu.VMEM((SUBLANES, BC_WIDTH), F32),
                        pltpu.VMEM((MAMBA_GROUPS, MAMBA_STATE, GROUP_WIDTH), F32),
                        pltpu.VMEM((L, MAMBA_INNER), F32), pltpu.VMEM((L, MAMBA_INNER), F32)],
        compiler_params=pltpu.CompilerParams(
            dimension_semantics=("arbitrary", "arbitrary"), vmem_limit_bytes=VMEM_LIMIT),
        name="mamba2_ssd",
    )(proj3, proj3, proj3, proj3, *params)


ROUTE_E0 = N_EXPERT_GROUPS
RI_E1, RI_E2, RI_RANK1, RI_RANK2 = range(4)
BIG = 1e30


def _merge_kernel(ya_ref, yb_ref, gate_ref, x_ref, wro_ref, wmo_ref, wout_ref, g2_ref, wr_ref, br_ref,
                  xo_ref, h2_ref, ri_ref, rw_ref, cnt_ref, run_cnt):
    tm = x_ref.shape[0]
    i = pl.program_id(0)

    @pl.when(i == 0)
    def _():
        run_cnt[...] = jnp.zeros_like(run_cnt)

    o_a = _dot(ya_ref[...], wro_ref[...])
    o_b = _dot(yb_ref[...], wmo_ref[...])
    gates = _sigmoid(gate_ref[...].astype(F32))
    merged = gates[:, 0:D_MODEL] * o_a + gates[:, D_MODEL:2 * D_MODEL] * o_b
    x = x_ref[...] + _dot(merged.astype(BF16), wout_ref[...])
    xo_ref[...] = x
    ms = jnp.mean(x * x, axis=-1, keepdims=True)
    h2 = x * lax.rsqrt(ms + NORM_EPS) * g2_ref[...]
    for s in range(D_TILES):
        h2_ref[:, s, :] = h2[:, s * LANES:(s + 1) * LANES]

    logits = _dot(h2.astype(BF16), wr_ref[...]) + br_ref[...]
    lane = lax.broadcasted_iota(jnp.int32, (tm, LANES), 1)
    gmask = lane < N_EXPERT_GROUPS
    gl = jnp.where(gmask, logits, -BIG)
    gmax = jnp.max(gl, axis=-1, keepdims=True)
    gidx = jnp.min(jnp.where(gl == gmax, lane, LANES), axis=-1, keepdims=True)
    gsum = jnp.sum(jnp.where(gmask, jnp.exp(gl - gmax), 0.0), axis=-1, keepdims=True)
    g_p = 1.0 / gsum
    e_lo = ROUTE_E0 + gidx * EXPERTS_PER_GROUP
    emask = (lane >= e_lo) & (lane < e_lo + EXPERTS_PER_GROUP)
    el = jnp.where(emask, logits, -BIG)
    e1 = jnp.max(el, axis=-1, keepdims=True)
    i1 = jnp.min(jnp.where(el == e1, lane, LANES), axis=-1, keepdims=True)
    el2 = jnp.where(lane == i1, -BIG, el)
    e2 = jnp.max(el2, axis=-1, keepdims=True)
    i2 = jnp.min(jnp.where(el2 == e2, lane, LANES), axis=-1, keepdims=True)
    q = jnp.exp(e2 - e1)
    w1 = g_p / (1.0 + q)
    w2 = g_p * q / (1.0 + q)
    eid1 = i1 - ROUTE_E0
    eid2 = i2 - ROUTE_E0

    onehot = ((lane == eid1) | (lane == eid2)).astype(BF16)
    strict = (lax.broadcasted_iota(jnp.int32, (tm, tm), 0)
              > lax.broadcasted_iota(jnp.int32, (tm, tm), 1)).astype(BF16)
    before = _dot(strict, onehot) + run_cnt[0:1]
    rank1 = jnp.sum(jnp.where(lane == eid1, before, 0.0), axis=-1, keepdims=True).astype(jnp.int32)
    rank2 = jnp.sum(jnp.where(lane == eid2, before, 0.0), axis=-1, keepdims=True).astype(jnp.int32)
    total = run_cnt[0:1] + jnp.sum(onehot.astype(F32), axis=0, keepdims=True)
    run_cnt[0:1] = total
    cnt_ref[...] = jnp.broadcast_to(total, cnt_ref.shape)

    ri = jnp.where(lane == RI_E1, eid1, 0)
    ri = jnp.where(lane == RI_E2, eid2, ri)
    ri = jnp.where(lane == RI_RANK1, rank1, ri)
    ri = jnp.where(lane == RI_RANK2, rank2, ri)
    ri_ref[...] = ri
    rw_ref[...] = jnp.where(lane == 0, w1, jnp.where(lane == 1, w2, 0.0))


def _merge(ya, yb, proj, x2d, w_ro, w_mo, w_out, g2_row, w_r, b_r, tm):
    t = x2d.shape[0]
    full = lambda arr: pl.BlockSpec(arr.shape, lambda i: (0,) * arr.ndim)
    rows = lambda width: pl.BlockSpec((tm, width), lambda i: (i, 0))
    return pl.pallas_call(
        _merge_kernel,
        out_shape=(jax.ShapeDtypeStruct((t, D_MODEL), F32),
                   jax.ShapeDtypeStruct((t, D_TILES, LANES), F32),
                   jax.ShapeDtypeStruct((t, LANES), jnp.int32),
                   jax.ShapeDtypeStruct((t, LANES), F32),
                   jax.ShapeDtypeStruct((SUBLANES, LANES), F32)),
        grid=(t // tm,),
        in_specs=[rows(D_MODEL), rows(MAMBA_INNER),
                  pl.BlockSpec((tm, 2 * D_MODEL), lambda i: (i, COL_GATE // (2 * D_MODEL))),
                  rows(D_MODEL), full(w_ro), full(w_mo), full(w_out), full(g2_row), full(w_r), full(b_r)],
        out_specs=(rows(D_MODEL), pl.BlockSpec((tm, D_TILES, LANES), lambda i: (i, 0, 0)),
                   rows(LANES), rows(LANES), pl.BlockSpec((SUBLANES, LANES), lambda i: (0, 0))),
        scratch_shapes=[pltpu.VMEM((SUBLANES, LANES), F32)],
        compiler_params=pltpu.CompilerParams(
            dimension_semantics=("arbitrary",), vmem_limit_bytes=VMEM_LIMIT),
        name="merge_route",
    )(ya, yb, proj, x2d, w_ro, w_mo, w_out, g2_row, w_r, b_r)


def _dispatch_kernel(pos_hbm, h2_ref, xb_in, xb_out, idx_smem, idx_sem, row_sem):
    del xb_in
    tm = h2_ref.shape[0]
    i = pl.program_id(0)
    idx_copy = pltpu.make_async_copy(pos_hbm.at[i], idx_smem, idx_sem)
    idx_copy.start()
    idx_copy.wait()

    def row_copy(t, k):
        return pltpu.make_async_copy(h2_ref.at[t], xb_out.at[idx_smem[2 * t + k]], row_sem)

    def issue(t, carry):
        row_copy(t, 0).start()
        row_copy(t, 1).start()
        return carry

    def drain(t, carry):
        row_copy(t, 0).wait()
        row_copy(t, 1).wait()
        return carry

    lax.fori_loop(0, tm, issue, 0)
    lax.fori_loop(0, tm, drain, 0)


def _dispatch(pos2, h2, xb_init, tm):
    t = h2.shape[0]
    return pl.pallas_call(
        _dispatch_kernel,
        out_shape=jax.ShapeDtypeStruct(xb_init.shape, xb_init.dtype),
        grid=(t // tm,),
        in_specs=[pl.BlockSpec(memory_space=pl.ANY),
                  pl.BlockSpec((tm, D_TILES, LANES), lambda i: (i, 0, 0)),
                  pl.BlockSpec(memory_space=pl.ANY)],
        out_specs=pl.BlockSpec(memory_space=pl.ANY),
        scratch_shapes=[pltpu.SMEM((2 * tm,), jnp.int32), pltpu.SemaphoreType.DMA, pltpu.SemaphoreType.DMA],
        input_output_aliases={2: 0},
        compiler_params=pltpu.CompilerParams(
            dimension_semantics=("arbitrary",), vmem_limit_bytes=VMEM_LIMIT, has_side_effects=True),
        name="moe_dispatch",
    )(pos2, h2, xb_init)


def _ffn_kernel(be_ref, xb_ref, w1_ref, w3_ref, w2_ref, yb_ref):
    del be_ref
    x = jnp.concatenate([xb_ref[:, s, :] for s in range(D_TILES)], axis=1).astype(BF16)
    h1 = _dot(x, w1_ref[0])
    h3 = _dot(x, w3_ref[0])
    hid = (_silu(h1) * h3).astype(BF16)
    y = _dot(hid, w2_ref[0])
    for s in range(D_TILES):
        yb_ref[:, s, :] = y[:, s * LANES:(s + 1) * LANES]


def _expert_ffn(block_e, xb, w1, w3, w2):
    n_rows = xb.shape[0]
    br = EXPERT_ROWS
    grid_spec = pltpu.PrefetchScalarGridSpec(
        num_scalar_prefetch=1,
        grid=(n_rows // br,),
        in_specs=[pl.BlockSpec((br, D_TILES, LANES), lambda i, be: (i, 0, 0)),
                  pl.BlockSpec((1, D_MODEL, EXPERT_FF), lambda i, be: (be[i], 0, 0)),
                  pl.BlockSpec((1, D_MODEL, EXPERT_FF), lambda i, be: (be[i], 0, 0)),
                  pl.BlockSpec((1, EXPERT_FF, D_MODEL), lambda i, be: (be[i], 0, 0))],
        out_specs=pl.BlockSpec((br, D_TILES, LANES), lambda i, be: (i, 0, 0)),
    )
    return pl.pallas_call(
        _ffn_kernel,
        out_shape=jax.ShapeDtypeStruct(xb.shape, F32),
        grid_spec=grid_spec,
        compiler_params=pltpu.CompilerParams(
            dimension_semantics=("arbitrary",), vmem_limit_bytes=VMEM_LIMIT),
        name="expert_ffn",
    )(block_e, xb, w1, w3, w2)


def _combine_kernel(final_norm, pos_hbm, yb_hbm, x_ref, rw_ref, fg_ref, o_ref, idx_smem, buf, idx_sem, row_sem):
    tm = x_ref.shape[0]
    i = pl.program_id(0)
    idx_copy = pltpu.make_async_copy(pos_hbm.at[i], idx_smem, idx_sem)
    idx_copy.start()
    idx_copy.wait()

    def row_copy(t, k):
        return pltpu.make_async_copy(yb_hbm.at[idx_smem[2 * t + k]], buf.at[k, t], row_sem)

    def issue(t, carry):
        row_copy(t, 0).start()
        row_copy(t, 1).start()
        return carry

    def drain(t, carry):
        row_copy(t, 0).wait()
        row_copy(t, 1).wait()
        return carry

    lax.fori_loop(0, tm, issue, 0)
    lax.fori_loop(0, tm, drain, 0)

    rw = rw_ref[...]
    w1 = rw[:, 0:1]
    w2 = rw[:, 1:2]
    outs = []
    ssq = jnp.zeros((tm, 1), F32)
    for s in range(D_TILES):
        xs = x_ref[:, s * LANES:(s + 1) * LANES] + w1 * buf[0, :, s, :] + w2 * buf[1, :, s, :]
        outs.append(xs)
        if final_norm:
            ssq = ssq + jnp.sum(xs * xs, axis=-1, keepdims=True)
    if final_norm:
        scale = lax.rsqrt(ssq * (1.0 / D_MODEL) + NORM_EPS)
    for s in range(D_TILES):
        sl = slice(s * LANES, (s + 1) * LANES)
        o_ref[:, sl] = outs[s] * scale * fg_ref[:, sl] if final_norm else outs[s]


def _combine(pos2, yb, x2d, rw, final_g_row, final_norm, tm):
    t = x2d.shape[0]
    return pl.pallas_call(
        functools.partial(_combine_kernel, final_norm),
        out_shape=jax.ShapeDtypeStruct(x2d.shape, F32),
        grid=(t // tm,),
        in_specs=[pl.BlockSpec(memory_space=pl.ANY), pl.BlockSpec(memory_space=pl.ANY),
                  pl.BlockSpec((tm, D_MODEL), lambda i: (i, 0)),
                  pl.BlockSpec((tm, LANES), lambda i: (i, 0)),
                  pl.BlockSpec((1, D_MODEL), lambda i: (0, 0))],
        out_specs=pl.BlockSpec((tm, D_MODEL), lambda i: (i, 0)),
        scratch_shapes=[pltpu.SMEM((2 * tm,), jnp.int32), pltpu.VMEM((2, tm, D_TILES, LANES), F32),
                        pltpu.SemaphoreType.DMA, pltpu.SemaphoreType.DMA],
        compiler_params=pltpu.CompilerParams(
            dimension_semantics=("arbitrary",), vmem_limit_bytes=VMEM_LIMIT),
        name="moe_combine",
    )(pos2, yb, x2d, rw, final_g_row)


def _pad_cols(w, width):
    return jnp.pad(w, ((0, 0), (0, width - w.shape[1])))


def _pack_w_in(w_in_l, w_v1_l):
    rw = 3 * D_MODEL
    o_wlo = rw
    o_alo = o_wlo + DECAY_LORA
    o_glo = o_alo + ICLR_LORA
    o_m = o_glo + GATE_LORA
    o_xbc = o_m + MAMBA_INNER
    o_dt = o_xbc + MAMBA_INNER + BC_WIDTH
    o_gate = o_dt + MAMBA_HEADS
    d = w_in_l.shape[0]
    vlo = w_v1_l if w_v1_l is not None else jnp.zeros((d, VALUE_LORA), w_in_l.dtype)
    dtv = _pad_cols(jnp.concatenate([w_in_l[:, o_dt:o_dt + MAMBA_HEADS], vlo], axis=1), LANES)
    cols = [
        w_in_l[:, o_m:o_m + MAMBA_INNER],
        w_in_l[:, o_gate:o_gate + 2 * D_MODEL],
        w_in_l[:, o_xbc:o_xbc + MAMBA_INNER],
        w_in_l[:, 0:rw],
        w_in_l[:, o_xbc + MAMBA_INNER:o_xbc + MAMBA_INNER + BC_WIDTH],
        w_in_l[:, o_wlo:o_glo],
        w_in_l[:, o_glo:o_m],
        dtv,
        jnp.zeros((d, N_COLS - COL_DTV - LANES), w_in_l.dtype),
    ]
    return jnp.concatenate(cols, axis=1).astype(BF16)


def _pair_major(row):
    return row.reshape(N_PAIRS, LANES)


def _pack_rwkv_params(l, mu_rwkv, w0, a0, v0, k_k, k_a, r_k, lnx_g, lnx_b, w_decay2, w_iclr2, w_gate2,
                      mu_v, w_v2):
    mu = mu_rwkv[l]
    has_vres = l > 0
    zeros = jnp.zeros((D_MODEL,), F32)
    rows = [mu[0:D_MODEL], mu[D_MODEL:2 * D_MODEL], mu[2 * D_MODEL:3 * D_MODEL], w0[l], a0[l],
            v0[l - 1] if has_vres else zeros, k_k[l], k_a[l], r_k[l].reshape(-1), lnx_g[l], lnx_b[l]]
    rows += [zeros] * (RP_ROWS - len(rows))
    rowp = jnp.stack([_pair_major(r) for r in rows], axis=1)
    rw = 3 * D_MODEL
    mu_vl = jnp.zeros((LANES,), F32)
    if has_vres:
        mu_vl = mu_vl.at[VLO_LANE:VLO_LANE + VALUE_LORA].set(mu_v[l - 1])
    smallp = jnp.stack([mu[rw:rw + LANES], mu[rw + LANES:rw + 2 * LANES], mu_vl]
                       + [jnp.zeros((LANES,), F32)] * (SUBLANES - 3), axis=0)
    z = lambda n: jnp.zeros((n, D_MODEL), F32)
    wd = jnp.concatenate([w_decay2[l], z(LANES - DECAY_LORA)], axis=0)
    wi = jnp.concatenate([z(DECAY_LORA), w_iclr2[l]], axis=0)
    wg = w_gate2[l]
    if has_vres:
        wv = jnp.concatenate([z(VLO_LANE), w_v2[l - 1], z(LANES - VLO_LANE - VALUE_LORA)], axis=0)
    else:
        wv = z(LANES)
    lora = jnp.stack([w.reshape(LANES, N_PAIRS, LANES).transpose(1, 0, 2) for w in (wd, wi, wg, wv)], axis=0)
    return rowp, smallp, lora.astype(BF16)


def _pack_mamba_params(l, conv_w, conv_b, dt_bias, a_log, d_skip, mnorm_g):
    cw = conv_w[l].astype(F32)
    cb = conv_b[l].reshape(1, -1)
    cw_xs = jnp.concatenate([cw[:, :MAMBA_INNER], jnp.zeros((SUBLANES - MAMBA_CONV, MAMBA_INNER), F32)], axis=0)
    cw_bc = jnp.concatenate([cw[:, MAMBA_INNER:], jnp.zeros((SUBLANES - MAMBA_CONV, BC_WIDTH), F32)], axis=0)
    pad = lambda v: jnp.pad(v, (0, LANES - MAMBA_HEADS))
    dtp = jnp.stack([pad(dt_bias[l]), pad(-jnp.exp(a_log[l].astype(F32)))]
                    + [jnp.zeros((LANES,), F32)] * (SUBLANES - 2), axis=0)
    dskip_x = jnp.repeat(d_skip[l], MAMBA_HEAD).reshape(1, MAMBA_INNER)
    expand = (jnp.arange(LANES)[:, None] == (jnp.arange(MAMBA_INNER)[None, :] // MAMBA_HEAD)).astype(BF16)
    return cw_xs, cb[:, :MAMBA_INNER], cw_bc, cb[:, MAMBA_INNER:], dtp, dskip_x, mnorm_g[l].reshape(1, -1), expand


def _route_positions(ri, cnt, n_tiles, tile):
    counts = cnt[0, :N_EXPERTS].astype(jnp.int32)
    pcounts = (counts + EXPERT_ROWS - 1) // EXPERT_ROWS * EXPERT_ROWS
    pends = jnp.cumsum(pcounts)
    poffsets = pends - pcounts
    pos = poffsets[ri[:, RI_E1:RI_E2 + 1]] + ri[:, RI_RANK1:RI_RANK2 + 1]
    return pos.reshape(n_tiles, 2 * tile), pends


def kernel(x, norm1_g, w_in, mu_rwkv, w0, w_decay2, a0, w_iclr2, w_gate2, k_k, k_a, r_k, lnx_g, lnx_b, w_rwkv_o, w_v1, mu_v, v0, w_v2, conv_w, conv_b, dt_bias, a_log, d_skip, mnorm_g, w_mamba_o, w_out, norm2_g, w_rg, b_rg, w_re, b_re, w_e1, w_e3, w_e2, final_g):
    bsz, seq, d = x.shape
    t = bsz * seq
    depth = w_in.shape[0]
    n_assign = 2 * t
    n_blocks = -(-(n_assign + N_EXPERTS * (EXPERT_ROWS - 1)) // EXPERT_ROWS)
    n_rows = n_blocks * EXPERT_ROWS
    tm_proj = min(1024, t)
    tm_merge = min(256, t)
    tm_moe = min(256, t)

    x2d = x.reshape(t, d)
    v_first = None
    for l in range(depth):
        w_cat = _pack_w_in(w_in[l], w_v1[l - 1] if l > 0 else None)
        proj = _inproj(x2d, norm1_g[l].reshape(1, d), w_cat, tm_proj, 512)
        proj3 = proj.reshape(bsz, seq, N_COLS)
        rowp, smallp, lora = _pack_rwkv_params(l, mu_rwkv, w0, a0, v0, k_k, k_a, r_k, lnx_g, lnx_b,
                                               w_decay2, w_iclr2, w_gate2, mu_v, w_v2)
        if l == 0:
            ya, v_first = _rwkv_branch(proj3, None, rowp, smallp, lora)
        else:
            ya = _rwkv_branch(proj3, v_first, rowp, smallp, lora)
        yb = _mamba_branch(proj3, *_pack_mamba_params(l, conv_w, conv_b, dt_bias, a_log, d_skip, mnorm_g))

        w_r = _pad_cols(jnp.concatenate([w_rg[l], w_re[l]], axis=1), LANES).astype(BF16)
        b_r = _pad_cols(jnp.concatenate([b_rg[l], b_re[l]]).reshape(1, -1), LANES)
        x2d, h2, ri, rw, cnt = _merge(
            ya.reshape(t, D_MODEL), yb.reshape(t, MAMBA_INNER), proj, x2d,
            w_rwkv_o[l].astype(BF16), w_mamba_o[l].astype(BF16), w_out[l].astype(BF16),
            norm2_g[l].reshape(1, d), w_r, b_r, tm_merge)

        pos2, pends = _route_positions(ri, cnt, t // tm_moe, tm_moe)
        block_start = jnp.arange(n_blocks, dtype=jnp.int32) * EXPERT_ROWS
        block_e = jnp.minimum(jnp.searchsorted(pends, block_start, side='right'), N_EXPERTS - 1).astype(jnp.int32)
        xb = _dispatch(pos2, h2, jnp.zeros((n_rows, D_TILES, LANES), F32), tm_moe)
        ybuf = _expert_ffn(block_e, xb, w_e1[l].astype(BF16), w_e3[l].astype(BF16), w_e2[l].astype(BF16))
        x2d = _combine(pos2, ybuf, x2d, rw, final_g.reshape(1, d), l == depth - 1, tm_moe)
    return x2d.reshape(bsz, seq, d)
```

```python
import functools

import jax
import jax.numpy as jnp
from jax import lax
from jax.experimental import pallas as pl
from jax.experimental.pallas import tpu as pltpu

F32 = jnp.float32
BF16 = jnp.bfloat16

D_MODEL = 1024
RWKV_HEAD = 64
DECAY_LORA = 64
ICLR_LORA = 64
VALUE_LORA = 32
GATE_LORA = 128
RWKV_GN_EPS = 64e-5
MAMBA_INNER = 2048
MAMBA_HEAD = 64
MAMBA_HEADS = 32
MAMBA_GROUPS = 4
MAMBA_STATE = 128
MAMBA_CONV = 4
SSD_CHUNK = 128
N_EXPERT_GROUPS = 4
EXPERTS_PER_GROUP = 8
N_EXPERTS = 32
EXPERT_FF = 512
NORM_EPS = 1e-6
MAMBA_NORM_EPS = 1e-5

LANES = 128
SUBLANES = 8
D_TILES = D_MODEL // LANES
N_PAIRS = D_MODEL // LANES
WKV_CHUNK = 64
EXPERT_ROWS = 256

COL_Z = 0
COL_GATE = 2048
COL_XS = 4096
COL_R = 6144
COL_K = 7168
COL_V = 8192
COL_BC = 9216
COL_WA = 10240
COL_GLO = 10368
COL_DTV = 10496
N_COLS = 10752
VLO_LANE = 32
INPROJ_TN = N_COLS // 7

VMEM_LIMIT = 56 * 1024 * 1024


def _softplus(x):
    return jnp.maximum(x, 0.0) + jnp.log(1.0 + jnp.exp(-jnp.abs(x)))


def _sigmoid(x):
    return 1.0 / (1.0 + jnp.exp(-x))


def _dot(a, b):
    return jnp.dot(a, b, preferred_element_type=F32)


def _dot_nt(a, b):
    return lax.dot_general(a, b, (((1,), (1,)), ((), ())), preferred_element_type=F32)


def _split_hi_lo(x):
    hi = x.astype(BF16)
    lo = (x - hi.astype(F32)).astype(BF16)
    return hi, lo


def _inproj_kernel(x_ref, g_ref, w_ref, o_ref, h_scr):
    @pl.when(pl.program_id(1) == 0)
    def _():
        x = x_ref[...]
        ms = jnp.mean(x * x, axis=-1, keepdims=True)
        h_scr[...] = (x * lax.rsqrt(ms + NORM_EPS) * g_ref[...]).astype(BF16)

    o_ref[...] = _dot(h_scr[...], w_ref[...]).astype(o_ref.dtype)


def _inproj(x2d, g_row, w_bf16, tm, tn):
    t, d = x2d.shape
    nc = w_bf16.shape[1]
    return pl.pallas_call(
        _inproj_kernel,
        out_shape=jax.ShapeDtypeStruct((t, nc), BF16),
        grid=(t // tm, nc // tn),
        in_specs=[
            pl.BlockSpec((tm, d), lambda i, j: (i, 0)),
            pl.BlockSpec((1, d), lambda i, j: (0, 0)),
            pl.BlockSpec((d, tn), lambda i, j: (0, j)),
        ],
        out_specs=pl.BlockSpec((tm, tn), lambda i, j: (i, j)),
        scratch_shapes=[pltpu.VMEM((tm, d), BF16)],
        compiler_params=pltpu.CompilerParams(
            dimension_semantics=("arbitrary", "arbitrary"), vmem_limit_bytes=VMEM_LIMIT),
        name="inproj",
    )(x2d, g_row, w_bf16)


RP_MU_R, RP_MU_K, RP_MU_V, RP_W0, RP_A0, RP_V0, RP_KK, RP_KA, RP_RK, RP_LNG, RP_LNB = range(11)
RP_ROWS = 16
SP_MU_WA, SP_MU_G, SP_MU_VL = range(3)
LR_DECAY, LR_ICLR, LR_GATE, LR_VAL = range(4)


def _bd_stack(x, m0):
    zero = jnp.zeros_like(x)
    return jnp.concatenate([jnp.where(m0, x, zero), jnp.where(m0, zero, x)], axis=0)


def _rwkv_kernel(has_vres, *refs):
    if has_vres:
        (r_ref, k_ref, v_ref, wa_ref, gl_ref, dtv_ref, vf_ref, rowp_ref, smallp_ref, lora_ref,
         y_ref, c_rkv, c_small, state, s_r, s_k, s_v, s_vf, s_small, s_y) = refs
        vout_ref = None
    else:
        (r_ref, k_ref, v_ref, wa_ref, gl_ref, rowp_ref, smallp_ref, lora_ref,
         y_ref, vout_ref, c_rkv, c_small, state, s_r, s_k, s_v, s_small, s_y) = refs
        dtv_ref = vf_ref = s_vf = None
    L = WKV_CHUNK
    c = pl.program_id(1)
    first = c == 0
    row = lax.broadcasted_iota(jnp.int32, (L, LANES), 0)
    lane = lax.broadcasted_iota(jnp.int32, (L, LANES), 1)
    row0 = row == 0

    @pl.when(first)
    def _():
        state[...] = jnp.zeros_like(state)
        c_rkv[...] = jnp.zeros_like(c_rkv)
        c_small[...] = jnp.zeros_like(c_small)

    def shift_mix(u, prev_row, mu):
        prev = jnp.where(row0, prev_row, pltpu.roll(u, 1, 0))
        return u + (prev - u) * mu

    for p in range(N_PAIRS):
        sl = slice(p * LANES, (p + 1) * LANES)
        rp = rowp_ref[p]
        for idx, (src, dst, mu_slot) in enumerate(((r_ref, s_r, RP_MU_R), (k_ref, s_k, RP_MU_K),
                                                   (v_ref, s_v, RP_MU_V))):
            u = src[0, :, sl].astype(F32)
            prev_row = c_rkv[idx, 0:1, sl]
            mixed = shift_mix(u, prev_row, rp[mu_slot:mu_slot + 1])
            c_rkv[idx, 0:1, sl] = u[L - 1:L]
            dst[p] = mixed
            if idx == 2 and vout_ref is not None:
                vout_ref[0, :, sl] = mixed.astype(vout_ref.dtype)
        if has_vres:
            s_vf[p] = vf_ref[0, :, sl].astype(F32)
    small_srcs = [(wa_ref, SP_MU_WA), (gl_ref, SP_MU_G)]
    if has_vres:
        small_srcs.append((dtv_ref, SP_MU_VL))
    for idx, (src, mu_slot) in enumerate(small_srcs):
        u = src[0].astype(F32)
        prev_row = c_small[idx, 0:1]
        s_small[idx] = shift_mix(u, prev_row, smallp_ref[mu_slot:mu_slot + 1])
        c_small[idx, 0:1] = u[L - 1:L]

    xwa = s_small[0]
    tanh_wa = jnp.tanh(xwa).astype(BF16)
    xwa_bf = xwa.astype(BF16)
    sig_g = _sigmoid(s_small[1]).astype(BF16)
    xvl_bf = s_small[2].astype(BF16) if has_vres else None

    m0 = lane < RWKV_HEAD
    s_in = lane % RWKV_HEAD
    strict = s_in < row
    incl = s_in <= row
    eye_w = (s_in == row).astype(F32)
    lane2 = lax.broadcasted_iota(jnp.int32, (2 * L, LANES), 1)
    row2 = lax.broadcasted_iota(jnp.int32, (2 * L, LANES), 0)
    bd_mask = (lane2 // RWKV_HEAD) == (row2 // RWKV_HEAD)
    bd_ones = bd_mask.astype(BF16)
    tril = (lax.broadcasted_iota(jnp.int32, (L, L), 0)
            >= lax.broadcasted_iota(jnp.int32, (L, L), 1)).astype(BF16)

    def pair_steps(p):
        rp = rowp_ref[p]
        r = s_r[p]
        xk = s_k[p]
        v = s_v[p]
        w_in = rp[RP_W0:RP_W0 + 1] + _dot(tanh_wa, lora_ref[LR_DECAY, p])
        a_in = rp[RP_A0:RP_A0 + 1] + _dot(xwa_bf, lora_ref[LR_ICLR, p])
        g = _dot(sig_g, lora_ref[LR_GATE, p])
        if has_vres:
            mix_in = rp[RP_V0:RP_V0 + 1] + _dot(xvl_bf, lora_ref[LR_VAL, p])
        kkr = xk * rp[RP_KK:RP_KK + 1]
        ss = _dot((kkr * kkr).astype(BF16), bd_ones)
        yield
        lw = -jnp.exp(-_softplus(-w_in) - 0.5)
        a = _sigmoid(a_in)
        if has_vres:
            v = v + (s_vf[p] - v) * _sigmoid(mix_in)
        kkn = kkr / jnp.maximum(jnp.sqrt(ss), 1e-12)
        k2 = xk * (1.0 + (a - 1.0) * rp[RP_KA:RP_KA + 1])
        avec = -kkn
        bvec = kkn * a
        bonus_s = _dot((r * k2 * rp[RP_RK:RP_RK + 1]).astype(BF16), bd_ones)

        lw_hi, lw_lo = _split_hi_lo(lw)
        w_cum = _dot(tril, lw_hi) + _dot(tril, lw_lo)
        yield
        w_mid = w_cum[L // 2 - 1:L // 2]
        w_end = w_cum[L - 1:L]
        e_abs = jnp.exp(w_cum)
        e_prev = jnp.exp(w_cum - lw)
        e_from_mid = jnp.exp(w_mid - w_cum)
        e_to_end = jnp.exp(w_end - w_cum)
        e_mid_inv = jnp.exp(-w_mid)
        r_abs = r * e_abs
        a_abs = avec * e_prev
        r_mid = r_abs * e_mid_inv
        a_mid = a_abs * e_mid_inv
        b_mid = bvec * e_from_mid
        k_mid = k2 * e_from_mid
        b_end = bvec * e_to_end
        k_end = k2 * e_to_end

        st = state[p]
        lhs_abs = jnp.concatenate([a_abs, r_abs], axis=0).astype(BF16)
        m1 = _dot_nt(lhs_abs, st.astype(BF16))
        lhs_mid = jnp.concatenate([a_mid, r_mid], axis=0).astype(BF16)
        rhs_mid = jnp.concatenate([_bd_stack(b_mid, m0), _bd_stack(k_mid, m0)], axis=0).astype(BF16)
        m2 = _dot_nt(lhs_mid, rhs_mid)
        yield
        a_ab = jnp.where(strict, m2[0:L, 0:LANES], 0.0)
        a_ak = jnp.where(strict, m2[0:L, LANES:2 * LANES], 0.0)
        a_rb = jnp.where(incl, m2[L:2 * L, 0:LANES], 0.0)
        a_rk = jnp.where(incl, m2[L:2 * L, LANES:2 * LANES], 0.0)

        v_bd = _bd_stack(v, m0).astype(BF16)
        rhs_u = m1[0:L] + _dot(a_ak.astype(BF16), v_bd)

        x_inv = eye_w + a_ab
        pw = _dot(a_ab.astype(BF16), _bd_stack(a_ab, m0).astype(BF16))
        yield
        for _ in range(4):
            res = _dot(jnp.concatenate([x_inv, pw], axis=0).astype(BF16), _bd_stack(pw, m0).astype(BF16))
            yield
            x_inv = x_inv + res[0:L]
            pw = res[L:2 * L]
        x_upd = _dot(x_inv.astype(BF16), _bd_stack(pw, m0).astype(BF16))
        yield
        x_inv = x_inv + x_upd

        u = _dot(x_inv.astype(BF16), _bd_stack(rhs_u, m0).astype(BF16))
        yield
        u_bd = _bd_stack(u, m0).astype(BF16)
        y = m1[L:2 * L] + _dot(jnp.concatenate([a_rb, a_rk], axis=1).astype(BF16),
                               jnp.concatenate([u_bd, v_bd], axis=0))

        uv_t = jnp.concatenate([u, v], axis=0).T.astype(BF16)
        upd = _dot(uv_t, jnp.concatenate([b_end, k_end], axis=0).astype(BF16))
        yield
        state[p] = st * jnp.exp(w_end) + jnp.where(bd_mask, upd, 0.0)

        inv_n = 1.0 / RWKV_HEAD
        mean = _dot(y.astype(BF16), bd_ones) * inv_n
        yield
        yc = y - mean
        var = _dot((yc * yc).astype(BF16), bd_ones) * inv_n
        yield
        yn = yc * lax.rsqrt(var + RWKV_GN_EPS) * rp[RP_LNG:RP_LNG + 1] + rp[RP_LNB:RP_LNB + 1]
        s_y[p] = (yn + bonus_s * v) * g

    chains = [pair_steps(p) for p in range(N_PAIRS)]
    while chains:
        alive = []
        for chain in chains:
            try:
                next(chain)
                alive.append(chain)
            except StopIteration:
                pass
        chains = alive
    for p in range(N_PAIRS):
        y_ref[0, :, p * LANES:(p + 1) * LANES] = s_y[p].astype(y_ref.dtype)


def _rwkv_branch(proj3, v_first, rowp, smallp, lora):
    b, s, _ = proj3.shape
    L = WKV_CHUNK
    has_vres = v_first is not None
    wide = lambda col: pl.BlockSpec((1, L, D_MODEL), lambda i, j, c=col // D_MODEL: (i, j, c))
    narrow = lambda col: pl.BlockSpec((1, L, LANES), lambda i, j, c=col // LANES: (i, j, c))
    full = lambda shape: pl.BlockSpec(shape, lambda i, j: (0,) * len(shape))
    out_blk = pl.BlockSpec((1, L, D_MODEL), lambda i, j: (i, j, 0))
    in_specs = [wide(COL_R), wide(COL_K), wide(COL_V), narrow(COL_WA), narrow(COL_GLO)]
    args = [proj3] * 5
    if has_vres:
        in_specs += [narrow(COL_DTV), out_blk]
        args += [proj3, v_first]
    in_specs += [full(rowp.shape), full(smallp.shape), full(lora.shape)]
    args += [rowp, smallp, lora]
    pm = lambda: pltpu.VMEM((N_PAIRS, L, LANES), F32)
    scratch = [pltpu.VMEM((3, SUBLANES, D_MODEL), F32), pltpu.VMEM((3, SUBLANES, LANES), F32),
               pltpu.VMEM((N_PAIRS, 2 * RWKV_HEAD, LANES), F32), pm(), pm(), pm()]
    if has_vres:
        scratch.append(pm())
    scratch += [pltpu.VMEM((3, L, LANES), F32), pm()]
    y_shape = jax.ShapeDtypeStruct((b, s, D_MODEL), BF16)
    out_shape = y_shape if has_vres else (y_shape, y_shape)
    out_specs = out_blk if has_vres else (out_blk, out_blk)
    return pl.pallas_call(
        functools.partial(_rwkv_kernel, has_vres),
        out_shape=out_shape,
        grid=(b, s // L),
        in_specs=in_specs,
        out_specs=out_specs,
        scratch_shapes=scratch,
        compiler_params=pltpu.CompilerParams(
            dimension_semantics=("arbitrary", "arbitrary"), vmem_limit_bytes=VMEM_LIMIT),
        name="rwkv7",
    )(*args)


HEADS_PER_GROUP = MAMBA_HEADS // MAMBA_GROUPS
GROUP_WIDTH = MAMBA_INNER // MAMBA_GROUPS
BC_WIDTH = 2 * MAMBA_GROUPS * MAMBA_STATE
DP_DT_BIAS, DP_A_HEAD = range(2)


def _silu(x):
    return x * _sigmoid(x)


def _mamba_kernel(z_ref, xs_ref, bc_ref, dtv_ref, cw_xs_ref, cb_xs_ref, cw_bc_ref, cb_bc_ref, dtp_ref,
                  dskip_ref, mnorm_ref, expand_ref, y_ref, tail_xs, tail_bc, state, s_xc, s_y):
    L = SSD_CHUNK
    c = pl.program_id(1)
    first = c == 0

    @pl.when(first)
    def _():
        state[...] = jnp.zeros_like(state)
        tail_xs[...] = jnp.zeros_like(tail_xs)
        tail_bc[...] = jnp.zeros_like(tail_bc)

    def conv_silu(u, tail_ref, w_ref, b_ref):
        width = u.shape[1]
        row8 = lax.broadcasted_iota(jnp.int32, (SUBLANES, width), 0)
        tail = tail_ref[...]
        acc = u * w_ref[MAMBA_CONV - 1:MAMBA_CONV] + b_ref[...]
        for d in range(1, MAMBA_CONV):
            rolled = pltpu.roll(u, d, 0)
            head = jnp.where(row8 < d, pltpu.roll(tail, d, 0), rolled[0:SUBLANES])
            shifted = jnp.concatenate([head, rolled[SUBLANES:]], axis=0)
            acc = acc + shifted * w_ref[MAMBA_CONV - 1 - d:MAMBA_CONV - d]
        tail_ref[...] = u[L - SUBLANES:L]
        return _silu(acc)

    xc = conv_silu(xs_ref[0].astype(F32), tail_xs, cw_xs_ref, cb_xs_ref)
    bcc = conv_silu(bc_ref[0].astype(F32), tail_bc, cw_bc_ref, cb_bc_ref)
    s_xc[...] = xc

    dt = _softplus(dtv_ref[0].astype(F32) + dtp_ref[DP_DT_BIAS:DP_DT_BIAS + 1])
    adt = dt * dtp_ref[DP_A_HEAD:DP_A_HEAD + 1]
    tril_b = lax.broadcasted_iota(jnp.int32, (L, L), 0) >= lax.broadcasted_iota(jnp.int32, (L, L), 1)
    tril = tril_b.astype(BF16)
    adt_hi, adt_lo = _split_hi_lo(adt)
    a_cs = _dot(tril, adt_hi) + _dot(tril, adt_lo)
    a_cs_t = a_cs.T
    exp_acs = jnp.exp(a_cs)
    to_end = jnp.exp(a_cs[L - 1:L] - a_cs)
    expand = expand_ref[...]
    dt_x = _dot(dt.astype(BF16), expand)
    exp_acs_x = _dot(exp_acs.astype(BF16), expand)
    to_end_x = _dot(to_end.astype(BF16), expand)
    xdt = xc * dt_x
    xdt_bf = xdt.astype(BF16)
    xw_bf = (xdt * to_end_x).astype(BF16)
    lane = lax.broadcasted_iota(jnp.int32, (L, LANES), 1)
    m0 = lane < MAMBA_HEAD

    for g in range(MAMBA_GROUPS):
        bg = bcc[:, g * MAMBA_STATE:(g + 1) * MAMBA_STATE]
        cg = bcc[:, (MAMBA_GROUPS + g) * MAMBA_STATE:(MAMBA_GROUPS + g + 1) * MAMBA_STATE]
        bg_bf = bg.astype(BF16)
        cg_bf = cg.astype(BF16)
        cb = _dot_nt(cg_bf, bg_bf)
        gs = slice(g * GROUP_WIDTH, (g + 1) * GROUP_WIDTH)
        st = state[g]
        y_off = _dot(cg_bf, st.astype(BF16)) * exp_acs_x[:, gs]
        for jp in range(HEADS_PER_GROUP // 2):
            ws = []
            for hh in range(2):
                j = g * HEADS_PER_GROUP + 2 * jp + hh
                seg = a_cs[:, j:j + 1] - a_cs_t[j:j + 1, :]
                dec = jnp.where(tril_b, jnp.exp(jnp.minimum(seg, 0.0)), 0.0)
                ws.append((cb * dec).astype(BF16))
            ps = slice(g * GROUP_WIDTH + jp * LANES, g * GROUP_WIDTH + (jp + 1) * LANES)
            y_diag = _dot(jnp.concatenate(ws, axis=1), _bd_stack(xdt_bf[:, ps], m0))
            s_y[:, ps] = y_diag + y_off[:, jp * LANES:(jp + 1) * LANES]
        state[g] = st * exp_acs_x[L - 1:L, gs] + _dot(bg.T.astype(BF16), xw_bf[:, gs])

    y = s_y[...] + s_xc[...] * dskip_ref[...]
    y = y * _silu(z_ref[0].astype(F32))
    for g in range(MAMBA_GROUPS):
        gs = slice(g * GROUP_WIDTH, (g + 1) * GROUP_WIDTH)
        yg = y[:, gs]
        ms = jnp.mean(yg * yg, axis=-1, keepdims=True)
        y_ref[0, :, gs] = (yg * lax.rsqrt(ms + MAMBA_NORM_EPS) * mnorm_ref[:, gs]).astype(y_ref.dtype)


def _mamba_branch(proj3, cw_xs, cb_xs, cw_bc, cb_bc, dtp, dskip_x, mnorm, expand):
    b, s, _ = proj3.shape
    L = SSD_CHUNK
    blk = lambda width, col: pl.BlockSpec((1, L, width), lambda i, j, c=col // width: (i, j, c))
    full = lambda arr: pl.BlockSpec(arr.shape, lambda i, j: (0,) * arr.ndim)
    params = [cw_xs, cb_xs, cw_bc, cb_bc, dtp, dskip_x, mnorm, expand]
    return pl.pallas_call(
        _mamba_kernel,
        out_shape=jax.ShapeDtypeStruct((b, s, MAMBA_INNER), BF16),
        grid=(b, s // L),
        in_specs=[blk(MAMBA_INNER, COL_Z), blk(MAMBA_INNER, COL_XS), blk(BC_WIDTH, COL_BC),
                  blk(LANES, COL_DTV)] + [full(a) for a in params],
        out_specs=pl.BlockSpec((1, L, MAMBA_INNER), lambda i, j: (i, j, 0)),
        scratch_shapes=[pltpu.VMEM((SUBLANES, MAMBA_INNER), F32), pltpu.VMEM((SUBLANES, BC_WIDTH), F32),
                        pltpu.VMEM((MAMBA_GROUPS, MAMBA_STATE, GROUP_WIDTH), F32),
                        pltpu.VMEM((L, MAMBA_INNER), F32), pltpu.VMEM((L, MAMBA_INNER), F32)],
        compiler_params=pltpu.CompilerParams(
            dimension_semantics=("arbitrary", "arbitrary"), vmem_limit_bytes=VMEM_LIMIT),
        name="mamba2_ssd",
    )(proj3, proj3, proj3, proj3, *params)


ROUTE_E0 = N_EXPERT_GROUPS
RI_E1, RI_E2, RI_RANK1, RI_RANK2 = range(4)
BIG = 1e30


def _merge_kernel(ya_ref, yb_ref, gate_ref, x_ref, wro_ref, wmo_ref, wout_ref, g2_ref, wr_ref, br_ref,
                  xo_ref, h2_ref, ri_ref, rw_ref, cnt_ref, run_cnt):
    tm = x_ref.shape[0]
    i = pl.program_id(0)

    @pl.when(i == 0)
    def _():
        run_cnt[...] = jnp.zeros_like(run_cnt)

    o_a = _dot(ya_ref[...], wro_ref[...])
    o_b = _dot(yb_ref[...], wmo_ref[...])
    gates = _sigmoid(gate_ref[...].astype(F32))
    merged = gates[:, 0:D_MODEL] * o_a + gates[:, D_MODEL:2 * D_MODEL] * o_b
    x = x_ref[...] + _dot(merged.astype(BF16), wout_ref[...])
    xo_ref[...] = x
    ms = jnp.mean(x * x, axis=-1, keepdims=True)
    h2 = x * lax.rsqrt(ms + NORM_EPS) * g2_ref[...]
    for s in range(D_TILES):
        h2_ref[:, s, :] = h2[:, s * LANES:(s + 1) * LANES]

    logits = _dot(h2.astype(BF16), wr_ref[...]) + br_ref[...]
    lane = lax.broadcasted_iota(jnp.int32, (tm, LANES), 1)
    gmask = lane < N_EXPERT_GROUPS
    gl = jnp.where(gmask, logits, -BIG)
    gmax = jnp.max(gl, axis=-1, keepdims=True)
    gidx = jnp.min(jnp.where(gl == gmax, lane, LANES), axis=-1, keepdims=True)
    gsum = jnp.sum(jnp.where(gmask, jnp.exp(gl - gmax), 0.0), axis=-1, keepdims=True)
    g_p = 1.0 / gsum
    e_lo = ROUTE_E0 + gidx * EXPERTS_PER_GROUP
    emask = (lane >= e_lo) & (lane < e_lo + EXPERTS_PER_GROUP)
    el = jnp.where(emask, logits, -BIG)
    e1 = jnp.max(el, axis=-1, keepdims=True)
    i1 = jnp.min(jnp.where(el == e1, lane, LANES), axis=-1, keepdims=True)
    el2 = jnp.where(lane == i1, -BIG, el)
    e2 = jnp.max(el2, axis=-1, keepdims=True)
    i2 = jnp.min(jnp.where(el2 == e2, lane, LANES), axis=-1, keepdims=True)
    q = jnp.exp(e2 - e1)
    w1 = g_p / (1.0 + q)
    w2 = g_p * q / (1.0 + q)
    eid1 = i1 - ROUTE_E0
    eid2 = i2 - ROUTE_E0

    onehot = ((lane == eid1) | (lane == eid2)).astype(BF16)
    strict = (lax.broadcasted_iota(jnp.int32, (tm, tm), 0)
              > lax.broadcasted_iota(jnp.int32, (tm, tm), 1)).astype(BF16)
    before = _dot(strict, onehot) + run_cnt[0:1]
    rank1 = jnp.sum(jnp.where(lane == eid1, before, 0.0), axis=-1, keepdims=True).astype(jnp.int32)
    rank2 = jnp.sum(jnp.where(lane == eid2, before, 0.0), axis=-1, keepdims=True).astype(jnp.int32)
    total = run_cnt[0:1] + jnp.sum(onehot.astype(F32), axis=0, keepdims=True)
    run_cnt[0:1] = total
    cnt_ref[...] = jnp.broadcast_to(total, cnt_ref.shape)

    ri = jnp.where(lane == RI_E1, eid1, 0)
    ri = jnp.where(lane == RI_E2, eid2, ri)
    ri = jnp.where(lane == RI_RANK1, rank1, ri)
    ri = jnp.where(lane == RI_RANK2, rank2, ri)
    ri_ref[...] = ri
    rw_ref[...] = jnp.where(lane == 0, w1, jnp.where(lane == 1, w2, 0.0))


def _merge(ya, yb, proj, x2d, w_ro, w_mo, w_out, g2_row, w_r, b_r, tm):
    t = x2d.shape[0]
    full = lambda arr: pl.BlockSpec(arr.shape, lambda i: (0,) * arr.ndim)
    rows = lambda width: pl.BlockSpec((tm, width), lambda i: (i, 0))
    return pl.pallas_call(
        _merge_kernel,
        out_shape=(jax.ShapeDtypeStruct((t, D_MODEL), F32),
                   jax.ShapeDtypeStruct((t, D_TILES, LANES), F32),
                   jax.ShapeDtypeStruct((t, LANES), jnp.int32),
                   jax.ShapeDtypeStruct((t, LANES), F32),
                   jax.ShapeDtypeStruct((SUBLANES, LANES), F32)),
        grid=(t // tm,),
        in_specs=[rows(D_MODEL), rows(MAMBA_INNER),
                  pl.BlockSpec((tm, 2 * D_MODEL), lambda i: (i, COL_GATE // (2 * D_MODEL))),
                  rows(D_MODEL), full(w_ro), full(w_mo), full(w_out), full(g2_row), full(w_r), full(b_r)],
        out_specs=(rows(D_MODEL), pl.BlockSpec((tm, D_TILES, LANES), lambda i: (i, 0, 0)),
                   rows(LANES), rows(LANES), pl.BlockSpec((SUBLANES, LANES), lambda i: (0, 0))),
        scratch_shapes=[pltpu.VMEM((SUBLANES, LANES), F32)],
        compiler_params=pltpu.CompilerParams(
            dimension_semantics=("arbitrary",), vmem_limit_bytes=VMEM_LIMIT),
        name="merge_route",
    )(ya, yb, proj, x2d, w_ro, w_mo, w_out, g2_row, w_r, b_r)


def _dispatch_kernel(pos_hbm, h2_ref, xb_in, xb_out, idx_smem, idx_sem, row_sem):
    del xb_in
    tm = h2_ref.shape[0]
    i = pl.program_id(0)
    idx_copy = pltpu.make_async_copy(pos_hbm.at[i], idx_smem, idx_sem)
    idx_copy.start()
    idx_copy.wait()

    def row_copy(t, k):
        return pltpu.make_async_copy(h2_ref.at[t], xb_out.at[idx_smem[2 * t + k]], row_sem)

    def issue(t, carry):
        row_copy(t, 0).start()
        row_copy(t, 1).start()
        return carry

    def drain(t, carry):
        row_copy(t, 0).wait()
        row_copy(t, 1).wait()
        return carry

    lax.fori_loop(0, tm, issue, 0)
    lax.fori_loop(0, tm, drain, 0)


def _dispatch(pos2, h2, xb_init, tm):
    t = h2.shape[0]
    return pl.pallas_call(
        _dispatch_kernel,
        out_shape=jax.ShapeDtypeStruct(xb_init.shape, xb_init.dtype),
        grid=(t // tm,),
        in_specs=[pl.BlockSpec(memory_space=pl.ANY),
                  pl.BlockSpec((tm, D_TILES, LANES), lambda i: (i, 0, 0)),
                  pl.BlockSpec(memory_space=pl.ANY)],
        out_specs=pl.BlockSpec(memory_space=pl.ANY),
        scratch_shapes=[pltpu.SMEM((2 * tm,), jnp.int32), pltpu.SemaphoreType.DMA, pltpu.SemaphoreType.DMA],
        input_output_aliases={2: 0},
        compiler_params=pltpu.CompilerParams(
            dimension_semantics=("arbitrary",), vmem_limit_bytes=VMEM_LIMIT, has_side_effects=True),
        name="moe_dispatch",
    )(pos2, h2, xb_init)


def _ffn_kernel(be_ref, xb_ref, w1_ref, w3_ref, w2_ref, yb_ref):
    del be_ref
    x = jnp.concatenate([xb_ref[:, s, :] for s in range(D_TILES)], axis=1).astype(BF16)
    h1 = _dot(x, w1_ref[0])
    h3 = _dot(x, w3_ref[0])
    hid = (_silu(h1) * h3).astype(BF16)
    y = _dot(hid, w2_ref[0])
    for s in range(D_TILES):
        yb_ref[:, s, :] = y[:, s * LANES:(s + 1) * LANES]


def _expert_ffn(block_e, xb, w1, w3, w2):
    n_rows = xb.shape[0]
    br = EXPERT_ROWS
    grid_spec = pltpu.PrefetchScalarGridSpec(
        num_scalar_prefetch=1,
        grid=(n_rows // br,),
        in_specs=[pl.BlockSpec((br, D_TILES, LANES), lambda i, be: (i, 0, 0)),
                  pl.BlockSpec((1, D_MODEL, EXPERT_FF), lambda i, be: (be[i], 0, 0)),
                  pl.BlockSpec((1, D_MODEL, EXPERT_FF), lambda i, be: (be[i], 0, 0)),
                  pl.BlockSpec((1, EXPERT_FF, D_MODEL), lambda i, be: (be[i], 0, 0))],
        out_specs=pl.BlockSpec((br, D_TILES, LANES), lambda i, be: (i, 0, 0)),
    )
    return pl.pallas_call(
        _ffn_kernel,
        out_shape=jax.ShapeDtypeStruct(xb.shape, F32),
        grid_spec=grid_spec,
        compiler_params=pltpu.CompilerParams(
            dimension_semantics=("arbitrary",), vmem_limit_bytes=VMEM_LIMIT),
        name="expert_ffn",
    )(block_e, xb, w1, w3, w2)


def _combine_kernel(final_norm, pos_hbm, yb_hbm, x_ref, rw_ref, fg_ref, o_ref, idx_smem, buf, idx_sem, row_sem):
    tm = x_ref.shape[0]
    i = pl.program_id(0)
    idx_copy = pltpu.make_async_copy(pos_hbm.at[i], idx_smem, idx_sem)
    idx_copy.start()
    idx_copy.wait()

    def row_copy(t, k):
        return pltpu.make_async_copy(yb_hbm.at[idx_smem[2 * t + k]], buf.at[k, t], row_sem)

    def issue(t, carry):
        row_copy(t, 0).start()
        row_copy(t, 1).start()
        return carry

    def drain(t, carry):
        row_copy(t, 0).wait()
        row_copy(t, 1).wait()
        return carry

    lax.fori_loop(0, tm, issue, 0)
    lax.fori_loop(0, tm, drain, 0)

    rw = rw_ref[...]
    w1 = rw[:, 0:1]
    w2 = rw[:, 1:2]
    outs = []
    ssq = jnp.zeros((tm, 1), F32)
    for s in range(D_TILES):
        xs = x_ref[:, s * LANES:(s + 1) * LANES] + w1 * buf[0, :, s, :] + w2 * buf[1, :, s, :]
        outs.append(xs)
        if final_norm:
            ssq = ssq + jnp.sum(xs * xs, axis=-1, keepdims=True)
    if final_norm:
        scale = lax.rsqrt(ssq * (1.0 / D_MODEL) + NORM_EPS)
    for s in range(D_TILES):
        sl = slice(s * LANES, (s + 1) * LANES)
        o_ref[:, sl] = outs[s] * scale * fg_ref[:, sl] if final_norm else outs[s]


def _combine(pos2, yb, x2d, rw, final_g_row, final_norm, tm):
    t = x2d.shape[0]
    return pl.pallas_call(
        functools.partial(_combine_kernel, final_norm),
        out_shape=jax.ShapeDtypeStruct(x2d.shape, F32),
        grid=(t // tm,),
        in_specs=[pl.BlockSpec(memory_space=pl.ANY), pl.BlockSpec(memory_space=pl.ANY),
                  pl.BlockSpec((tm, D_MODEL), lambda i: (i, 0)),
                  pl.BlockSpec((tm, LANES), lambda i: (i, 0)),
                  pl.BlockSpec((1, D_MODEL), lambda i: (0, 0))],
        out_specs=pl.BlockSpec((tm, D_MODEL), lambda i: (i, 0)),
        scratch_shapes=[pltpu.SMEM((2 * tm,), jnp.int32), pltpu.VMEM((2, tm, D_TILES, LANES), F32),
                        pltpu.SemaphoreType.DMA, pltpu.SemaphoreType.DMA],
        compiler_params=pltpu.CompilerParams(
            dimension_semantics=("arbitrary",), vmem_limit_bytes=VMEM_LIMIT),
        name="moe_combine",
    )(pos2, yb, x2d, rw, final_g_row)


def _pad_cols(w, width):
    return jnp.pad(w, ((0, 0), (0, width - w.shape[1])))


def _pack_w_in(w_in_l, w_v1_l):
    rw = 3 * D_MODEL
    o_wlo = rw
    o_alo = o_wlo + DECAY_LORA
    o_glo = o_alo + ICLR_LORA
    o_m = o_glo + GATE_LORA
    o_xbc = o_m + MAMBA_INNER
    o_dt = o_xbc + MAMBA_INNER + BC_WIDTH
    o_gate = o_dt + MAMBA_HEADS
    d = w_in_l.shape[0]
    vlo = w_v1_l if w_v1_l is not None else jnp.zeros((d, VALUE_LORA), w_in_l.dtype)
    dtv = _pad_cols(jnp.concatenate([w_in_l[:, o_dt:o_dt + MAMBA_HEADS], vlo], axis=1), LANES)
    cols = [
        w_in_l[:, o_m:o_m + MAMBA_INNER],
        w_in_l[:, o_gate:o_gate + 2 * D_MODEL],
        w_in_l[:, o_xbc:o_xbc + MAMBA_INNER],
        w_in_l[:, 0:rw],
        w_in_l[:, o_xbc + MAMBA_INNER:o_xbc + MAMBA_INNER + BC_WIDTH],
        w_in_l[:, o_wlo:o_glo],
        w_in_l[:, o_glo:o_m],
        dtv,
        jnp.zeros((d, N_COLS - COL_DTV - LANES), w_in_l.dtype),
    ]
    return jnp.concatenate(cols, axis=1).astype(BF16)


def _pair_major(row):
    return row.reshape(N_PAIRS, LANES)


def _pack_rwkv_params(l, mu_rwkv, w0, a0, v0, k_k, k_a, r_k, lnx_g, lnx_b, w_decay2, w_iclr2, w_gate2,
                      mu_v, w_v2):
    mu = mu_rwkv[l]
    has_vres = l > 0
    zeros = jnp.zeros((D_MODEL,), F32)
    rows = [mu[0:D_MODEL], mu[D_MODEL:2 * D_MODEL], mu[2 * D_MODEL:3 * D_MODEL], w0[l], a0[l],
            v0[l - 1] if has_vres else zeros, k_k[l], k_a[l], r_k[l].reshape(-1), lnx_g[l], lnx_b[l]]
    rows += [zeros] * (RP_ROWS - len(rows))
    rowp = jnp.stack([_pair_major(r) for r in rows], axis=1)
    rw = 3 * D_MODEL
    mu_vl = jnp.zeros((LANES,), F32)
    if has_vres:
        mu_vl = mu_vl.at[VLO_LANE:VLO_LANE + VALUE_LORA].set(mu_v[l - 1])
    smallp = jnp.stack([mu[rw:rw + LANES], mu[rw + LANES:rw + 2 * LANES], mu_vl]
                       + [jnp.zeros((LANES,), F32)] * (SUBLANES - 3), axis=0)
    z = lambda n: jnp.zeros((n, D_MODEL), F32)
    wd = jnp.concatenate([w_decay2[l], z(LANES - DECAY_LORA)], axis=0)
    wi = jnp.concatenate([z(DECAY_LORA), w_iclr2[l]], axis=0)
    wg = w_gate2[l]
    if has_vres:
        wv = jnp.concatenate([z(VLO_LANE), w_v2[l - 1], z(LANES - VLO_LANE - VALUE_LORA)], axis=0)
    else:
        wv = z(LANES)
    lora = jnp.stack([w.reshape(LANES, N_PAIRS, LANES).transpose(1, 0, 2) for w in (wd, wi, wg, wv)], axis=0)
    return rowp, smallp, lora.astype(BF16)


def _pack_mamba_params(l, conv_w, conv_b, dt_bias, a_log, d_skip, mnorm_g):
    cw = conv_w[l].astype(F32)
    cb = conv_b[l].reshape(1, -1)
    cw_xs = jnp.concatenate([cw[:, :MAMBA_INNER], jnp.zeros((SUBLANES - MAMBA_CONV, MAMBA_INNER), F32)], axis=0)
    cw_bc = jnp.concatenate([cw[:, MAMBA_INNER:], jnp.zeros((SUBLANES - MAMBA_CONV, BC_WIDTH), F32)], axis=0)
    pad = lambda v: jnp.pad(v, (0, LANES - MAMBA_HEADS))
    dtp = jnp.stack([pad(dt_bias[l]), pad(-jnp.exp(a_log[l].astype(F32)))]
                    + [jnp.zeros((LANES,), F32)] * (SUBLANES - 2), axis=0)
    dskip_x = jnp.repeat(d_skip[l], MAMBA_HEAD).reshape(1, MAMBA_INNER)
    expand = (jnp.arange(LANES)[:, None] == (jnp.arange(MAMBA_INNER)[None, :] // MAMBA_HEAD)).astype(BF16)
    return cw_xs, cb[:, :MAMBA_INNER], cw_bc, cb[:, MAMBA_INNER:], dtp, dskip_x, mnorm_g[l].reshape(1, -1), expand


def _route_positions(ri, cnt, n_tiles, tile):
    counts = cnt[0, :N_EXPERTS].astype(jnp.int32)
    pcounts = (counts + EXPERT_ROWS - 1) // EXPERT_ROWS * EXPERT_ROWS
    pends = jnp.cumsum(pcounts)
    poffsets = pends - pcounts
    pos = poffsets[ri[:, RI_E1:RI_E2 + 1]] + ri[:, RI_RANK1:RI_RANK2 + 1]
    return pos.reshape(n_tiles, 2 * tile), pends


def kernel(x, norm1_g, w_in, mu_rwkv, w0, w_decay2, a0, w_iclr2, w_gate2, k_k, k_a, r_k, lnx_g, lnx_b, w_rwkv_o, w_v1, mu_v, v0, w_v2, conv_w, conv_b, dt_bias, a_log, d_skip, mnorm_g, w_mamba_o, w_out, norm2_g, w_rg, b_rg, w_re, b_re, w_e1, w_e3, w_e2, final_g):
    bsz, seq, d = x.shape
    t = bsz * seq
    depth = w_in.shape[0]
    n_assign = 2 * t
    n_blocks = -(-(n_assign + N_EXPERTS * (EXPERT_ROWS - 1)) // EXPERT_ROWS)
    n_rows = n_blocks * EXPERT_ROWS
    tm_proj = min(1024, t)
    tm_merge = min(256, t)
    tm_moe = min(256, t)

    x2d = x.reshape(t, d)
    v_first = None
    for l in range(depth):
        w_cat = _pack_w_in(w_in[l], w_v1[l - 1] if l > 0 else None)
        proj = _inproj(x2d, norm1_g[l].reshape(1, d), w_cat, tm_proj, INPROJ_TN)
        proj3 = proj.reshape(bsz, seq, N_COLS)
        rowp, smallp, lora = _pack_rwkv_params(l, mu_rwkv, w0, a0, v0, k_k, k_a, r_k, lnx_g, lnx_b,
                                               w_decay2, w_iclr2, w_gate2, mu_v, w_v2)
        if l == 0:
            ya, v_first = _rwkv_branch(proj3, None, rowp, smallp, lora)
        else:
            ya = _rwkv_branch(proj3, v_first, rowp, smallp, lora)
        yb = _mamba_branch(proj3, *_pack_mamba_params(l, conv_w, conv_b, dt_bias, a_log, d_skip, mnorm_g))

        w_r = _pad_cols(jnp.concatenate([w_rg[l], w_re[l]], axis=1), LANES).astype(BF16)
        b_r = _pad_cols(jnp.concatenate([b_rg[l], b_re[l]]).reshape(1, -1), LANES)
        x2d, h2, ri, rw, cnt = _merge(
            ya.reshape(t, D_MODEL), yb.reshape(t, MAMBA_INNER), proj, x2d,
            w_rwkv_o[l].astype(BF16), w_mamba_o[l].astype(BF16), w_out[l].astype(BF16),
            norm2_g[l].reshape(1, d), w_r, b_r, tm_merge)

        pos2, pends = _route_positions(ri, cnt, t // tm_moe, tm_moe)
        block_start = jnp.arange(n_blocks, dtype=jnp.int32) * EXPERT_ROWS
        block_e = jnp.minimum(jnp.searchsorted(pends, block_start, side='right'), N_EXPERTS - 1).astype(jnp.int32)
        xb = _dispatch(pos2, h2, jnp.zeros((n_rows, D_TILES, LANES), F32), tm_moe)
        ybuf = _expert_ffn(block_e, xb, w_e1[l].astype(BF16), w_e3[l].astype(BF16), w_e2[l].astype(BF16))
        x2d = _combine(pos2, ybuf, x2d, rw, final_g.reshape(1, d), l == depth - 1, tm_moe)
    return x2d.reshape(bsz, seq, d)
```

```python
import functools

import jax
import jax.numpy as jnp
from jax import lax
from jax.experimental import pallas as pl
from jax.experimental.pallas import tpu as pltpu

F32 = jnp.float32
BF16 = jnp.bfloat16

D_MODEL = 1024
RWKV_HEAD = 64
DECAY_LORA = 64
ICLR_LORA = 64
VALUE_LORA = 32
GATE_LORA = 128
RWKV_GN_EPS = 64e-5
MAMBA_INNER = 2048
MAMBA_HEAD = 64
MAMBA_HEADS = 32
MAMBA_GROUPS = 4
MAMBA_STATE = 128
MAMBA_CONV = 4
SSD_CHUNK = 128
N_EXPERT_GROUPS = 4
EXPERTS_PER_GROUP = 8
N_EXPERTS = 32
EXPERT_FF = 512
NORM_EPS = 1e-6
MAMBA_NORM_EPS = 1e-5

LANES = 128
SUBLANES = 8
D_TILES = D_MODEL // LANES
N_PAIRS = D_MODEL // LANES
WKV_CHUNK = 64
EXPERT_ROWS = 256

COL_Z = 0
COL_GATE = 2048
COL_XS = 4096
COL_R = 6144
COL_K = 7168
COL_V = 8192
COL_BC = 9216
COL_WA = 10240
COL_GLO = 10368
COL_DTV = 10496
N_COLS = 10752
VLO_LANE = 32
INPROJ_TN = N_COLS // 7

VMEM_LIMIT = 56 * 1024 * 1024


def _softplus(x):
    return jnp.maximum(x, 0.0) + jnp.log(1.0 + jnp.exp(-jnp.abs(x)))


def _sigmoid(x):
    return 1.0 / (1.0 + jnp.exp(-x))


def _dot(a, b):
    return jnp.dot(a, b, preferred_element_type=F32)


def _dot_nt(a, b):
    return lax.dot_general(a, b, (((1,), (1,)), ((), ())), preferred_element_type=F32)


def _split_hi_lo(x):
    hi = x.astype(BF16)
    lo = (x - hi.astype(F32)).astype(BF16)
    return hi, lo


def _inproj_kernel(x_ref, g_ref, w_ref, o_ref, h_scr):
    @pl.when(pl.program_id(1) == 0)
    def _():
        x = x_ref[...]
        ms = jnp.mean(x * x, axis=-1, keepdims=True)
        h_scr[...] = (x * lax.rsqrt(ms + NORM_EPS) * g_ref[...]).astype(BF16)

    o_ref[...] = _dot(h_scr[...], w_ref[...]).astype(o_ref.dtype)


def _inproj(x2d, g_row, w_bf16, tm, tn):
    t, d = x2d.shape
    nc = w_bf16.shape[1]
    return pl.pallas_call(
        _inproj_kernel,
        out_shape=jax.ShapeDtypeStruct((t, nc), BF16),
        grid=(t // tm, nc // tn),
        in_specs=[
            pl.BlockSpec((tm, d), lambda i, j: (i, 0)),
            pl.BlockSpec((1, d), lambda i, j: (0, 0)),
            pl.BlockSpec((d, tn), lambda i, j: (0, j)),
        ],
        out_specs=pl.BlockSpec((tm, tn), lambda i, j: (i, j)),
        scratch_shapes=[pltpu.VMEM((tm, d), BF16)],
        compiler_params=pltpu.CompilerParams(
            dimension_semantics=("arbitrary", "arbitrary"), vmem_limit_bytes=VMEM_LIMIT),
        name="inproj",
    )(x2d, g_row, w_bf16)


RP_MU_R, RP_MU_K, RP_MU_V, RP_W0, RP_A0, RP_V0, RP_KK, RP_KA, RP_RK, RP_LNG, RP_LNB = range(11)
RP_ROWS = 16
SP_MU_WA, SP_MU_G, SP_MU_VL = range(3)
LR_DECAY, LR_ICLR, LR_GATE, LR_VAL = range(4)


def _bd_stack(x, m0):
    zero = jnp.zeros_like(x)
    return jnp.concatenate([jnp.where(m0, x, zero), jnp.where(m0, zero, x)], axis=0)


def _rwkv_kernel(has_vres, *refs):
    if has_vres:
        (r_ref, k_ref, v_ref, wa_ref, gl_ref, dtv_ref, vf_ref, rowp_ref, smallp_ref, lora_ref,
         y_ref, c_rkv, c_small, state, s_r, s_k, s_v, s_vf, s_small, s_y) = refs
        vout_ref = None
    else:
        (r_ref, k_ref, v_ref, wa_ref, gl_ref, rowp_ref, smallp_ref, lora_ref,
         y_ref, vout_ref, c_rkv, c_small, state, s_r, s_k, s_v, s_small, s_y) = refs
        dtv_ref = vf_ref = s_vf = None
    L = WKV_CHUNK
    c = pl.program_id(1)
    first = c == 0
    row = lax.broadcasted_iota(jnp.int32, (L, LANES), 0)
    lane = lax.broadcasted_iota(jnp.int32, (L, LANES), 1)
    row0 = row == 0

    @pl.when(first)
    def _():
        state[...] = jnp.zeros_like(state)
        c_rkv[...] = jnp.zeros_like(c_rkv)
        c_small[...] = jnp.zeros_like(c_small)

    def shift_mix(u, prev_row, mu):
        prev = jnp.where(row0, prev_row, pltpu.roll(u, 1, 0))
        return u + (prev - u) * mu

    for p in range(N_PAIRS):
        sl = slice(p * LANES, (p + 1) * LANES)
        rp = rowp_ref[p]
        for idx, (src, dst, mu_slot) in enumerate(((r_ref, s_r, RP_MU_R), (k_ref, s_k, RP_MU_K),
                                                   (v_ref, s_v, RP_MU_V))):
            u = src[0, :, sl].astype(F32)
            prev_row = c_rkv[idx, 0:1, sl]
            mixed = shift_mix(u, prev_row, rp[mu_slot:mu_slot + 1])
            c_rkv[idx, 0:1, sl] = u[L - 1:L]
            dst[p] = mixed
            if idx == 2 and vout_ref is not None:
                vout_ref[0, :, sl] = mixed.astype(vout_ref.dtype)
        if has_vres:
            s_vf[p] = vf_ref[0, :, sl].astype(F32)
    small_srcs = [(wa_ref, SP_MU_WA), (gl_ref, SP_MU_G)]
    if has_vres:
        small_srcs.append((dtv_ref, SP_MU_VL))
    for idx, (src, mu_slot) in enumerate(small_srcs):
        u = src[0].astype(F32)
        prev_row = c_small[idx, 0:1]
        s_small[idx] = shift_mix(u, prev_row, smallp_ref[mu_slot:mu_slot + 1])
        c_small[idx, 0:1] = u[L - 1:L]

    xwa = s_small[0]
    tanh_wa = jnp.tanh(xwa).astype(BF16)
    xwa_bf = xwa.astype(BF16)
    sig_g = _sigmoid(s_small[1]).astype(BF16)
    xvl_bf = s_small[2].astype(BF16) if has_vres else None

    m0 = lane < RWKV_HEAD
    s_in = lane % RWKV_HEAD
    strict = s_in < row
    incl = s_in <= row
    eye_w = (s_in == row).astype(F32)
    lane2 = lax.broadcasted_iota(jnp.int32, (2 * L, LANES), 1)
    row2 = lax.broadcasted_iota(jnp.int32, (2 * L, LANES), 0)
    bd_mask = (lane2 // RWKV_HEAD) == (row2 // RWKV_HEAD)
    bd_ones = bd_mask.astype(BF16)
    tril = (lax.broadcasted_iota(jnp.int32, (L, L), 0)
            >= lax.broadcasted_iota(jnp.int32, (L, L), 1)).astype(BF16)

    def pair_steps(p):
        rp = rowp_ref[p]
        r = s_r[p]
        xk = s_k[p]
        v = s_v[p]
        w_in = rp[RP_W0:RP_W0 + 1] + _dot(tanh_wa, lora_ref[LR_DECAY, p])
        a_in = rp[RP_A0:RP_A0 + 1] + _dot(xwa_bf, lora_ref[LR_ICLR, p])
        g = _dot(sig_g, lora_ref[LR_GATE, p])
        if has_vres:
            mix_in = rp[RP_V0:RP_V0 + 1] + _dot(xvl_bf, lora_ref[LR_VAL, p])
        kkr = xk * rp[RP_KK:RP_KK + 1]
        ss = _dot((kkr * kkr).astype(BF16), bd_ones)
        yield
        lw = -jnp.exp(-_softplus(-w_in) - 0.5)
        a = _sigmoid(a_in)
        if has_vres:
            v = v + (s_vf[p] - v) * _sigmoid(mix_in)
        kkn = kkr / jnp.maximum(jnp.sqrt(ss), 1e-12)
        k2 = xk * (1.0 + (a - 1.0) * rp[RP_KA:RP_KA + 1])
        avec = -kkn
        bvec = kkn * a
        bonus_s = _dot((r * k2 * rp[RP_RK:RP_RK + 1]).astype(BF16), bd_ones)

        lw_hi, lw_lo = _split_hi_lo(lw)
        w_cum = _dot(tril, lw_hi) + _dot(tril, lw_lo)
        yield
        w_mid = w_cum[L // 2 - 1:L // 2]
        w_end = w_cum[L - 1:L]
        e_abs = jnp.exp(w_cum)
        e_prev = jnp.exp(w_cum - lw)
        e_from_mid = jnp.exp(w_mid - w_cum)
        e_to_end = jnp.exp(w_end - w_cum)
        e_mid_inv = jnp.exp(-w_mid)
        r_abs = r * e_abs
        a_abs = avec * e_prev
        r_mid = r_abs * e_mid_inv
        a_mid = a_abs * e_mid_inv
        b_mid = bvec * e_from_mid
        k_mid = k2 * e_from_mid
        b_end = bvec * e_to_end
        k_end = k2 * e_to_end

        st = state[p]
        lhs_abs = jnp.concatenate([a_abs, r_abs], axis=0).astype(BF16)
        m1 = _dot_nt(lhs_abs, st.astype(BF16))
        lhs_mid = jnp.concatenate([a_mid, r_mid], axis=0).astype(BF16)
        rhs_mid = jnp.concatenate([_bd_stack(b_mid, m0), _bd_stack(k_mid, m0)], axis=0).astype(BF16)
        m2 = _dot_nt(lhs_mid, rhs_mid)
        yield
        a_ab = jnp.where(strict, m2[0:L, 0:LANES], 0.0)
        a_ak = jnp.where(strict, m2[0:L, LANES:2 * LANES], 0.0)
        a_rb = jnp.where(incl, m2[L:2 * L, 0:LANES], 0.0)
        a_rk = jnp.where(incl, m2[L:2 * L, LANES:2 * LANES], 0.0)

        v_bd = _bd_stack(v, m0).astype(BF16)
        rhs_u = m1[0:L] + _dot(a_ak.astype(BF16), v_bd)

        x_inv = eye_w + a_ab
        pw = _dot(a_ab.astype(BF16), _bd_stack(a_ab, m0).astype(BF16))
        yield
        for _ in range(4):
            res = _dot(jnp.concatenate([x_inv, pw], axis=0).astype(BF16), _bd_stack(pw, m0).astype(BF16))
            yield
            x_inv = x_inv + res[0:L]
            pw = res[L:2 * L]
        x_upd = _dot(x_inv.astype(BF16), _bd_stack(pw, m0).astype(BF16))
        yield
        x_inv = x_inv + x_upd

        u = _dot(x_inv.astype(BF16), _bd_stack(rhs_u, m0).astype(BF16))
        yield
        u_bd = _bd_stack(u, m0).astype(BF16)
        y = m1[L:2 * L] + _dot(jnp.concatenate([a_rb, a_rk], axis=1).astype(BF16),
                               jnp.concatenate([u_bd, v_bd], axis=0))

        uv_t = jnp.concatenate([u, v], axis=0).T.astype(BF16)
        upd = _dot(uv_t, jnp.concatenate([b_end, k_end], axis=0).astype(BF16))
        yield
        state[p] = st * jnp.exp(w_end) + jnp.where(bd_mask, upd, 0.0)

        inv_n = 1.0 / RWKV_HEAD
        mean = _dot(y.astype(BF16), bd_ones) * inv_n
        yield
        yc = y - mean
        var = _dot((yc * yc).astype(BF16), bd_ones) * inv_n
        yield
        yn = yc * lax.rsqrt(var + RWKV_GN_EPS) * rp[RP_LNG:RP_LNG + 1] + rp[RP_LNB:RP_LNB + 1]
        s_y[p] = (yn + bonus_s * v) * g

    chains = [pair_steps(p) for p in range(N_PAIRS)]
    while chains:
        alive = []
        for chain in chains:
            try:
                next(chain)
                alive.append(chain)
            except StopIteration:
                pass
        chains = alive
    for p in range(N_PAIRS):
        y_ref[0, :, p * LANES:(p + 1) * LANES] = s_y[p].astype(y_ref.dtype)


def _rwkv_branch(proj3, v_first, rowp, smallp, lora):
    b, s, _ = proj3.shape
    L = WKV_CHUNK
    has_vres = v_first is not None
    wide = lambda col: pl.BlockSpec((1, L, D_MODEL), lambda i, j, c=col // D_MODEL: (i, j, c))
    narrow = lambda col: pl.BlockSpec((1, L, LANES), lambda i, j, c=col // LANES: (i, j, c))
    full = lambda shape: pl.BlockSpec(shape, lambda i, j: (0,) * len(shape))
    out_blk = pl.BlockSpec((1, L, D_MODEL), lambda i, j: (i, j, 0))
    in_specs = [wide(COL_R), wide(COL_K), wide(COL_V), narrow(COL_WA), narrow(COL_GLO)]
    args = [proj3] * 5
    if has_vres:
        in_specs += [narrow(COL_DTV), out_blk]
        args += [proj3, v_first]
    in_specs += [full(rowp.shape), full(smallp.shape), full(lora.shape)]
    args += [rowp, smallp, lora]
    pm = lambda: pltpu.VMEM((N_PAIRS, L, LANES), F32)
    scratch = [pltpu.VMEM((3, SUBLANES, D_MODEL), F32), pltpu.VMEM((3, SUBLANES, LANES), F32),
               pltpu.VMEM((N_PAIRS, 2 * RWKV_HEAD, LANES), F32), pm(), pm(), pm()]
    if has_vres:
        scratch.append(pm())
    scratch += [pltpu.VMEM((3, L, LANES), F32), pm()]
    y_shape = jax.ShapeDtypeStruct((b, s, D_MODEL), BF16)
    out_shape = y_shape if has_vres else (y_shape, y_shape)
    out_specs = out_blk if has_vres else (out_blk, out_blk)
    return pl.pallas_call(
        functools.partial(_rwkv_kernel, has_vres),
        out_shape=out_shape,
        grid=(b, s // L),
        in_specs=in_specs,
        out_specs=out_specs,
        scratch_shapes=scratch,
        compiler_params=pltpu.CompilerParams(
            dimension_semantics=("arbitrary", "arbitrary"), vmem_limit_bytes=VMEM_LIMIT),
        name="rwkv7",
    )(*args)


HEADS_PER_GROUP = MAMBA_HEADS // MAMBA_GROUPS
GROUP_WIDTH = MAMBA_INNER // MAMBA_GROUPS
BC_WIDTH = 2 * MAMBA_GROUPS * MAMBA_STATE
DP_DT_BIAS, DP_A_HEAD = range(2)


def _silu(x):
    return x * _sigmoid(x)


def _mamba_kernel(z_ref, xs_ref, bc_ref, dtv_ref, cw_xs_ref, cb_xs_ref, cw_bc_ref, cb_bc_ref, dtp_ref,
                  dskip_ref, mnorm_ref, expand_ref, y_ref, tail_xs, tail_bc, state):
    L = SSD_CHUNK
    c = pl.program_id(1)
    first = c == 0

    @pl.when(first)
    def _():
        state[...] = jnp.zeros_like(state)
        tail_xs[...] = jnp.zeros_like(tail_xs)
        tail_bc[...] = jnp.zeros_like(tail_bc)

    shift_row = lax.broadcasted_iota(jnp.int32, (L, 2 * L), 0)
    shift_col = lax.broadcasted_iota(jnp.int32, (L, 2 * L), 1)
    shifts = [(shift_col == shift_row + (L - d)).astype(BF16) for d in range(1, MAMBA_CONV)]

    def conv_silu(src_ref, prev_ref, w_ref, b_ref, cols):
        u_bf = src_ref[0, :, cols]
        ucat = jnp.concatenate([prev_ref[:, cols], u_bf], axis=0)
        acc = u_bf.astype(F32) * w_ref[MAMBA_CONV - 1:MAMBA_CONV, cols] + b_ref[:, cols]
        for d in range(1, MAMBA_CONV):
            acc = acc + _dot(shifts[d - 1], ucat) * w_ref[MAMBA_CONV - 1 - d:MAMBA_CONV - d, cols]
        prev_ref[:, cols] = u_bf
        return _silu(acc)

    dt = _softplus(dtv_ref[0].astype(F32) + dtp_ref[DP_DT_BIAS:DP_DT_BIAS + 1])
    adt = dt * dtp_ref[DP_A_HEAD:DP_A_HEAD + 1]
    tril_b = lax.broadcasted_iota(jnp.int32, (L, L), 0) >= lax.broadcasted_iota(jnp.int32, (L, L), 1)
    tril = tril_b.astype(BF16)
    adt_hi, adt_lo = _split_hi_lo(adt)
    a_cs = _dot(tril, adt_hi) + _dot(tril, adt_lo)
    a_cs_t = a_cs.T
    dt_bf = dt.astype(BF16)
    exp_acs_bf = jnp.exp(a_cs).astype(BF16)
    to_end_bf = jnp.exp(a_cs[L - 1:L] - a_cs).astype(BF16)
    lane = lax.broadcasted_iota(jnp.int32, (L, LANES), 1)
    m0 = lane < MAMBA_HEAD

    for g in range(MAMBA_GROUPS):
        gs = slice(g * GROUP_WIDTH, (g + 1) * GROUP_WIDTH)
        b_cols = slice(g * MAMBA_STATE, (g + 1) * MAMBA_STATE)
        c_cols = slice((MAMBA_GROUPS + g) * MAMBA_STATE, (MAMBA_GROUPS + g + 1) * MAMBA_STATE)
        xc = conv_silu(xs_ref, tail_xs, cw_xs_ref, cb_xs_ref, gs)
        bg = conv_silu(bc_ref, tail_bc, cw_bc_ref, cb_bc_ref, b_cols)
        cg_bf = conv_silu(bc_ref, tail_bc, cw_bc_ref, cb_bc_ref, c_cols).astype(BF16)
        expand = expand_ref[:, gs]
        exp_acs_x = _dot(exp_acs_bf, expand)
        xdt = xc * _dot(dt_bf, expand)
        xdt_bf = xdt.astype(BF16)
        xw_bf = (xdt * _dot(to_end_bf, expand)).astype(BF16)
        cb = _dot_nt(cg_bf, bg.astype(BF16))
        st = state[g]
        y_off = _dot(cg_bf, st.astype(BF16)) * exp_acs_x
        y_parts = []
        for jp in range(HEADS_PER_GROUP // 2):
            ws = []
            for hh in range(2):
                j = g * HEADS_PER_GROUP + 2 * jp + hh
                seg = a_cs[:, j:j + 1] - a_cs_t[j:j + 1, :]
                dec = jnp.where(tril_b, jnp.exp(jnp.minimum(seg, 0.0)), 0.0)
                ws.append((cb * dec).astype(BF16))
            ps = slice(jp * LANES, (jp + 1) * LANES)
            y_parts.append(_dot(jnp.concatenate(ws, axis=1), _bd_stack(xdt_bf[:, ps], m0)))
        state[g] = st * exp_acs_x[L - 1:L] + _dot(bg.T.astype(BF16), xw_bf)
        y = jnp.concatenate(y_parts, axis=1) + y_off + xc * dskip_ref[:, gs]
        y = y * _silu(z_ref[0, :, gs].astype(F32))
        ms = jnp.mean(y * y, axis=-1, keepdims=True)
        y_ref[0, :, gs] = (y * lax.rsqrt(ms + MAMBA_NORM_EPS) * mnorm_ref[:, gs]).astype(y_ref.dtype)


def _mamba_branch(proj3, cw_xs, cb_xs, cw_bc, cb_bc, dtp, dskip_x, mnorm, expand):
    b, s, _ = proj3.shape
    L = SSD_CHUNK
    blk = lambda width, col: pl.BlockSpec((1, L, width), lambda i, j, c=col // width: (i, j, c))
    full = lambda arr: pl.BlockSpec(arr.shape, lambda i, j: (0,) * arr.ndim)
    params = [cw_xs, cb_xs, cw_bc, cb_bc, dtp, dskip_x, mnorm, expand]
    return pl.pallas_call(
        _mamba_kernel,
        out_shape=jax.ShapeDtypeStruct((b, s, MAMBA_INNER), BF16),
        grid=(b, s // L),
        in_specs=[blk(MAMBA_INNER, COL_Z), blk(MAMBA_INNER, COL_XS), blk(BC_WIDTH, COL_BC),
                  blk(LANES, COL_DTV)] + [full(a) for a in params],
        out_specs=pl.BlockSpec((1, L, MAMBA_INNER), lambda i, j: (i, j, 0)),
        scratch_shapes=[pltpu.VMEM((L, MAMBA_INNER), BF16), pltpu.VMEM((L, BC_WIDTH), BF16),
                        pltpu.VMEM((MAMBA_GROUPS, MAMBA_STATE, GROUP_WIDTH), F32)],
        compiler_params=pltpu.CompilerParams(
            dimension_semantics=("arbitrary", "arbitrary"), vmem_limit_bytes=VMEM_LIMIT),
        name="mamba2_ssd",
    )(proj3, proj3, proj3, proj3, *params)


ROUTE_E0 = N_EXPERT_GROUPS
RI_E1, RI_E2, RI_RANK1, RI_RANK2 = range(4)
BIG = 1e30


def _merge_kernel(ya_ref, yb_ref, gate_ref, x_ref, wro_ref, wmo_ref, wout_ref, g2_ref, wr_ref, br_ref,
                  xo_ref, h2_ref, ri_ref, rw_ref, cnt_ref, run_cnt):
    tm = x_ref.shape[0]
    i = pl.program_id(0)

    @pl.when(i == 0)
    def _():
        run_cnt[...] = jnp.zeros_like(run_cnt)

    o_a = _dot(ya_ref[...], wro_ref[...])
    o_b = _dot(yb_ref[...], wmo_ref[...])
    gates = _sigmoid(gate_ref[...].astype(F32))
    merged = gates[:, 0:D_MODEL] * o_a + gates[:, D_MODEL:2 * D_MODEL] * o_b
    x = x_ref[...] + _dot(merged.astype(BF16), wout_ref[...])
    xo_ref[...] = x
    ms = jnp.mean(x * x, axis=-1, keepdims=True)
    h2 = x * lax.rsqrt(ms + NORM_EPS) * g2_ref[...]
    h2_ref[...] = h2

    logits = _dot(h2.astype(BF16), wr_ref[...]) + br_ref[...]
    lane = lax.broadcasted_iota(jnp.int32, (tm, LANES), 1)
    gmask = lane < N_EXPERT_GROUPS
    gl = jnp.where(gmask, logits, -BIG)
    gmax = jnp.max(gl, axis=-1, keepdims=True)
    gidx = jnp.min(jnp.where(gl == gmax, lane, LANES), axis=-1, keepdims=True)
    gsum = jnp.sum(jnp.where(gmask, jnp.exp(gl - gmax), 0.0), axis=-1, keepdims=True)
    g_p = 1.0 / gsum
    e_lo = ROUTE_E0 + gidx * EXPERTS_PER_GROUP
    emask = (lane >= e_lo) & (lane < e_lo + EXPERTS_PER_GROUP)
    el = jnp.where(emask, logits, -BIG)
    e1 = jnp.max(el, axis=-1, keepdims=True)
    i1 = jnp.min(jnp.where(el == e1, lane, LANES), axis=-1, keepdims=True)
    el2 = jnp.where(lane == i1, -BIG, el)
    e2 = jnp.max(el2, axis=-1, keepdims=True)
    i2 = jnp.min(jnp.where(el2 == e2, lane, LANES), axis=-1, keepdims=True)
    q = jnp.exp(e2 - e1)
    w1 = g_p / (1.0 + q)
    w2 = g_p * q / (1.0 + q)
    eid1 = i1 - ROUTE_E0
    eid2 = i2 - ROUTE_E0

    onehot = ((lane == eid1) | (lane == eid2)).astype(BF16)
    strict = (lax.broadcasted_iota(jnp.int32, (tm, tm), 0)
              > lax.broadcasted_iota(jnp.int32, (tm, tm), 1)).astype(BF16)
    before = _dot(strict, onehot) + run_cnt[0:1]
    rank1 = jnp.sum(jnp.where(lane == eid1, before, 0.0), axis=-1, keepdims=True).astype(jnp.int32)
    rank2 = jnp.sum(jnp.where(lane == eid2, before, 0.0), axis=-1, keepdims=True).astype(jnp.int32)
    total = run_cnt[0:1] + jnp.sum(onehot.astype(F32), axis=0, keepdims=True)
    run_cnt[0:1] = total
    cnt_ref[...] = jnp.broadcast_to(total, cnt_ref.shape)

    ri = jnp.where(lane == RI_E1, eid1, 0)
    ri = jnp.where(lane == RI_E2, eid2, ri)
    ri = jnp.where(lane == RI_RANK1, rank1, ri)
    ri = jnp.where(lane == RI_RANK2, rank2, ri)
    ri_ref[...] = ri
    rw_ref[...] = jnp.where(lane == 0, w1, jnp.where(lane == 1, w2, 0.0))


def _merge(ya, yb, proj, x2d, w_ro, w_mo, w_out, g2_row, w_r, b_r, tm):
    t = x2d.shape[0]
    full = lambda arr: pl.BlockSpec(arr.shape, lambda i: (0,) * arr.ndim)
    rows = lambda width: pl.BlockSpec((tm, width), lambda i: (i, 0))
    return pl.pallas_call(
        _merge_kernel,
        out_shape=(jax.ShapeDtypeStruct((t, D_MODEL), F32),
                   jax.ShapeDtypeStruct((t, D_MODEL), F32),
                   jax.ShapeDtypeStruct((t, LANES), jnp.int32),
                   jax.ShapeDtypeStruct((t, LANES), F32),
                   jax.ShapeDtypeStruct((SUBLANES, LANES), F32)),
        grid=(t // tm,),
        in_specs=[rows(D_MODEL), rows(MAMBA_INNER),
                  pl.BlockSpec((tm, 2 * D_MODEL), lambda i: (i, COL_GATE // (2 * D_MODEL))),
                  rows(D_MODEL), full(w_ro), full(w_mo), full(w_out), full(g2_row), full(w_r), full(b_r)],
        out_specs=(rows(D_MODEL), rows(D_MODEL),
                   rows(LANES), rows(LANES), pl.BlockSpec((SUBLANES, LANES), lambda i: (0, 0))),
        scratch_shapes=[pltpu.VMEM((SUBLANES, LANES), F32)],
        compiler_params=pltpu.CompilerParams(
            dimension_semantics=("arbitrary",), vmem_limit_bytes=VMEM_LIMIT),
        name="merge_route",
    )(ya, yb, proj, x2d, w_ro, w_mo, w_out, g2_row, w_r, b_r)


ROW_UNROLL = 8


def _fetch_step_indices(pos_hbm, idx_smem, idx_sem):
    i = pl.program_id(0)
    n = pl.num_programs(0)
    slot = i % 2
    per_step = pos_hbm.shape[1]

    def copy(step, s):
        dst = idx_smem.at[pl.ds(pl.multiple_of(s * per_step, per_step), per_step)]
        return pltpu.make_async_copy(pos_hbm.at[step], dst, idx_sem.at[s])

    @pl.when(i == 0)
    def _():
        copy(0, 0).start()

    copy(i, slot).wait()

    @pl.when(i + 1 < n)
    def _():
        copy(i + 1, 1 - slot).start()

    return slot * per_step


def _issue_row_copies(row_copy, n_groups):
    def issue(g, carry):
        for u in range(SUBLANES):
            for k in range(2):
                row_copy(g, u, k).start(priority=k)
        return carry

    lax.fori_loop(0, n_groups, issue, 0)


def _dispatch_kernel(pos_hbm, h2_ref, xb_in, xb_out, idx_smem, idx_sem, row_sem):
    del xb_in
    n_groups = h2_ref.shape[0]
    tm = n_groups * SUBLANES
    base = _fetch_step_indices(pos_hbm, idx_smem, idx_sem)

    def row_copy(g, u, k):
        row = idx_smem[base + 2 * SUBLANES * g + (2 * u + k)]
        return pltpu.make_async_copy(h2_ref.at[g, pl.ds(u, 1)], xb_out.at[pl.ds(row, 1)], row_sem)

    _issue_row_copies(row_copy, n_groups)
    for _ in range(2):
        pltpu.make_async_copy(xb_out.at[pl.ds(0, tm)], xb_out.at[pl.ds(0, tm)], row_sem).wait()


def _dispatch(pos2, h2, xb_init, tm):
    t = h2.shape[0]
    return pl.pallas_call(
        _dispatch_kernel,
        out_shape=jax.ShapeDtypeStruct(xb_init.shape, xb_init.dtype),
        grid=(t // tm,),
        in_specs=[pl.BlockSpec(memory_space=pl.ANY),
                  pl.BlockSpec((tm // SUBLANES, SUBLANES, D_MODEL), lambda i: (i, 0, 0)),
                  pl.BlockSpec(memory_space=pl.ANY)],
        out_specs=pl.BlockSpec(memory_space=pl.ANY),
        scratch_shapes=[pltpu.SMEM((4 * tm,), jnp.int32), pltpu.SemaphoreType.DMA((2,)),
                        pltpu.SemaphoreType.DMA],
        input_output_aliases={2: 0},
        compiler_params=pltpu.CompilerParams(
            dimension_semantics=("arbitrary",), vmem_limit_bytes=VMEM_LIMIT, has_side_effects=True),
        name="moe_dispatch",
    )(pos2, h2.reshape(t // SUBLANES, SUBLANES, D_MODEL), xb_init)


def _ffn_kernel(be_ref, xb_ref, w1_ref, w3_ref, w2_ref, yb_ref):
    del be_ref
    x = xb_ref[...].astype(BF16)
    h1 = _dot(x, w1_ref[0])
    h3 = _dot(x, w3_ref[0])
    hid = (_silu(h1) * h3).astype(BF16)
    yb_ref[...] = _dot(hid, w2_ref[0])


def _expert_ffn(block_e, xb, w1, w3, w2):
    n_rows = xb.shape[0]
    br = EXPERT_ROWS
    grid_spec = pltpu.PrefetchScalarGridSpec(
        num_scalar_prefetch=1,
        grid=(n_rows // br,),
        in_specs=[pl.BlockSpec((br, D_MODEL), lambda i, be: (i, 0)),
                  pl.BlockSpec((1, D_MODEL, EXPERT_FF), lambda i, be: (be[i], 0, 0)),
                  pl.BlockSpec((1, D_MODEL, EXPERT_FF), lambda i, be: (be[i], 0, 0)),
                  pl.BlockSpec((1, EXPERT_FF, D_MODEL), lambda i, be: (be[i], 0, 0))],
        out_specs=pl.BlockSpec((br, D_MODEL), lambda i, be: (i, 0)),
    )
    return pl.pallas_call(
        _ffn_kernel,
        out_shape=jax.ShapeDtypeStruct(xb.shape, F32),
        grid_spec=grid_spec,
        compiler_params=pltpu.CompilerParams(
            dimension_semantics=("arbitrary",), vmem_limit_bytes=VMEM_LIMIT),
        name="expert_ffn",
    )(block_e, xb, w1, w3, w2)


def _combine_kernel(final_norm, pos_hbm, yb_hbm, x_ref, rw_ref, fg_ref, o_ref, idx_smem, buf, idx_sem, row_sem):
    tm = x_ref.shape[0]
    n_groups = tm // SUBLANES
    base = _fetch_step_indices(pos_hbm, idx_smem, idx_sem)

    def row_copy(g, u, k):
        row = idx_smem[base + 2 * SUBLANES * g + (2 * u + k)]
        return pltpu.make_async_copy(yb_hbm.at[pl.ds(row, 1)], buf.at[k, g, pl.ds(u, 1)], row_sem)

    _issue_row_copies(row_copy, n_groups)
    for _ in range(2):
        pltpu.make_async_copy(yb_hbm.at[pl.ds(0, tm)], yb_hbm.at[pl.ds(0, tm)], row_sem).wait()

    rw = rw_ref[...]
    y1 = buf[0].reshape(tm, D_MODEL)
    y2 = buf[1].reshape(tm, D_MODEL)
    x = x_ref[...] + rw[:, 0:1] * y1 + rw[:, 1:2] * y2
    if final_norm:
        ms = jnp.mean(x * x, axis=-1, keepdims=True)
        x = x * lax.rsqrt(ms + NORM_EPS) * fg_ref[...]
    o_ref[...] = x


def _combine(pos2, yb, x2d, rw, final_g_row, final_norm, tm):
    t = x2d.shape[0]
    return pl.pallas_call(
        functools.partial(_combine_kernel, final_norm),
        out_shape=jax.ShapeDtypeStruct(x2d.shape, F32),
        grid=(t // tm,),
        in_specs=[pl.BlockSpec(memory_space=pl.ANY), pl.BlockSpec(memory_space=pl.ANY),
                  pl.BlockSpec((tm, D_MODEL), lambda i: (i, 0)),
                  pl.BlockSpec((tm, LANES), lambda i: (i, 0)),
                  pl.BlockSpec((1, D_MODEL), lambda i: (0, 0))],
        out_specs=pl.BlockSpec((tm, D_MODEL), lambda i: (i, 0)),
        scratch_shapes=[pltpu.SMEM((4 * tm,), jnp.int32),
                        pltpu.VMEM((2, tm // SUBLANES, SUBLANES, D_MODEL), F32),
                        pltpu.SemaphoreType.DMA((2,)), pltpu.SemaphoreType.DMA],
        compiler_params=pltpu.CompilerParams(
            dimension_semantics=("arbitrary",), vmem_limit_bytes=VMEM_LIMIT),
        name="moe_combine",
    )(pos2, yb, x2d, rw, final_g_row)


def _pad_cols(w, width):
    return jnp.pad(w, ((0, 0), (0, width - w.shape[1])))


def _pack_w_in(w_in_l, w_v1_l):
    rw = 3 * D_MODEL
    o_wlo = rw
    o_alo = o_wlo + DECAY_LORA
    o_glo = o_alo + ICLR_LORA
    o_m = o_glo + GATE_LORA
    o_xbc = o_m + MAMBA_INNER
    o_dt = o_xbc + MAMBA_INNER + BC_WIDTH
    o_gate = o_dt + MAMBA_HEADS
    d = w_in_l.shape[0]
    vlo = w_v1_l if w_v1_l is not None else jnp.zeros((d, VALUE_LORA), w_in_l.dtype)
    dtv = _pad_cols(jnp.concatenate([w_in_l[:, o_dt:o_dt + MAMBA_HEADS], vlo], axis=1), LANES)
    cols = [
        w_in_l[:, o_m:o_m + MAMBA_INNER],
        w_in_l[:, o_gate:o_gate + 2 * D_MODEL],
        w_in_l[:, o_xbc:o_xbc + MAMBA_INNER],
        w_in_l[:, 0:rw],
        w_in_l[:, o_xbc + MAMBA_INNER:o_xbc + MAMBA_INNER + BC_WIDTH],
        w_in_l[:, o_wlo:o_glo],
        w_in_l[:, o_glo:o_m],
        dtv,
        jnp.zeros((d, N_COLS - COL_DTV - LANES), w_in_l.dtype),
    ]
    return jnp.concatenate(cols, axis=1).astype(BF16)


def _pair_major(row):
    return row.reshape(N_PAIRS, LANES)


def _pack_rwkv_params(l, mu_rwkv, w0, a0, v0, k_k, k_a, r_k, lnx_g, lnx_b, w_decay2, w_iclr2, w_gate2,
                      mu_v, w_v2):
    mu = mu_rwkv[l]
    has_vres = l > 0
    zeros = jnp.zeros((D_MODEL,), F32)
    rows = [mu[0:D_MODEL], mu[D_MODEL:2 * D_MODEL], mu[2 * D_MODEL:3 * D_MODEL], w0[l], a0[l],
            v0[l - 1] if has_vres else zeros, k_k[l], k_a[l], r_k[l].reshape(-1), lnx_g[l], lnx_b[l]]
    rows += [zeros] * (RP_ROWS - len(rows))
    rowp = jnp.stack([_pair_major(r) for r in rows], axis=1)
    rw = 3 * D_MODEL
    mu_vl = jnp.zeros((LANES,), F32)
    if has_vres:
        mu_vl = mu_vl.at[VLO_LANE:VLO_LANE + VALUE_LORA].set(mu_v[l - 1])
    smallp = jnp.stack([mu[rw:rw + LANES], mu[rw + LANES:rw + 2 * LANES], mu_vl]
                       + [jnp.zeros((LANES,), F32)] * (SUBLANES - 3), axis=0)
    z = lambda n: jnp.zeros((n, D_MODEL), F32)
    wd = jnp.concatenate([w_decay2[l], z(LANES - DECAY_LORA)], axis=0)
    wi = jnp.concatenate([z(DECAY_LORA), w_iclr2[l]], axis=0)
    wg = w_gate2[l]
    if has_vres:
        wv = jnp.concatenate([z(VLO_LANE), w_v2[l - 1], z(LANES - VLO_LANE - VALUE_LORA)], axis=0)
    else:
        wv = z(LANES)
    lora = jnp.stack([w.reshape(LANES, N_PAIRS, LANES).transpose(1, 0, 2) for w in (wd, wi, wg, wv)], axis=0)
    return rowp, smallp, lora.astype(BF16)


def _pack_mamba_params(l, conv_w, conv_b, dt_bias, a_log, d_skip, mnorm_g):
    cw = conv_w[l].astype(F32)
    cb = conv_b[l].reshape(1, -1)
    cw_xs = jnp.concatenate([cw[:, :MAMBA_INNER], jnp.zeros((SUBLANES - MAMBA_CONV, MAMBA_INNER), F32)], axis=0)
    cw_bc = jnp.concatenate([cw[:, MAMBA_INNER:], jnp.zeros((SUBLANES - MAMBA_CONV, BC_WIDTH), F32)], axis=0)
    pad = lambda v: jnp.pad(v, (0, LANES - MAMBA_HEADS))
    dtp = jnp.stack([pad(dt_bias[l]), pad(-jnp.exp(a_log[l].astype(F32)))]
                    + [jnp.zeros((LANES,), F32)] * (SUBLANES - 2), axis=0)
    dskip_x = jnp.repeat(d_skip[l], MAMBA_HEAD).reshape(1, MAMBA_INNER)
    expand = (jnp.arange(LANES)[:, None] == (jnp.arange(MAMBA_INNER)[None, :] // MAMBA_HEAD)).astype(BF16)
    return cw_xs, cb[:, :MAMBA_INNER], cw_bc, cb[:, MAMBA_INNER:], dtp, dskip_x, mnorm_g[l].reshape(1, -1), expand


def _route_positions(ri, cnt, n_tiles, tile):
    counts = cnt[0, :N_EXPERTS].astype(jnp.int32)
    pcounts = (counts + EXPERT_ROWS - 1) // EXPERT_ROWS * EXPERT_ROWS
    pends = jnp.cumsum(pcounts)
    poffsets = pends - pcounts
    eid = ri[:, RI_E1:RI_E2 + 1]
    onehot = eid[:, :, None] == jnp.arange(N_EXPERTS, dtype=jnp.int32)
    pos = jnp.sum(jnp.where(onehot, poffsets, 0), axis=-1) + ri[:, RI_RANK1:RI_RANK2 + 1]
    return pos.reshape(n_tiles, 2 * tile), pends


def kernel(x, norm1_g, w_in, mu_rwkv, w0, w_decay2, a0, w_iclr2, w_gate2, k_k, k_a, r_k, lnx_g, lnx_b, w_rwkv_o, w_v1, mu_v, v0, w_v2, conv_w, conv_b, dt_bias, a_log, d_skip, mnorm_g, w_mamba_o, w_out, norm2_g, w_rg, b_rg, w_re, b_re, w_e1, w_e3, w_e2, final_g):
    bsz, seq, d = x.shape
    t = bsz * seq
    depth = w_in.shape[0]
    n_assign = 2 * t
    n_blocks = -(-(n_assign + N_EXPERTS * (EXPERT_ROWS - 1)) // EXPERT_ROWS)
    n_rows = n_blocks * EXPERT_ROWS
    tm_proj = min(1024, t)
    tm_merge = min(256, t)
    tm_moe = min(512, t)

    x2d = x.reshape(t, d)
    v_first = None
    for l in range(depth):
        w_cat = _pack_w_in(w_in[l], w_v1[l - 1] if l > 0 else None)
        proj = _inproj(x2d, norm1_g[l].reshape(1, d), w_cat, tm_proj, INPROJ_TN)
        proj3 = proj.reshape(bsz, seq, N_COLS)
        rowp, smallp, lora = _pack_rwkv_params(l, mu_rwkv, w0, a0, v0, k_k, k_a, r_k, lnx_g, lnx_b,
                                               w_decay2, w_iclr2, w_gate2, mu_v, w_v2)
        if l == 0:
            ya, v_first = _rwkv_branch(proj3, None, rowp, smallp, lora)
        else:
            ya = _rwkv_branch(proj3, v_first, rowp, smallp, lora)
        yb = _mamba_branch(proj3, *_pack_mamba_params(l, conv_w, conv_b, dt_bias, a_log, d_skip, mnorm_g))

        w_r = _pad_cols(jnp.concatenate([w_rg[l], w_re[l]], axis=1), LANES).astype(BF16)
        b_r = _pad_cols(jnp.concatenate([b_rg[l], b_re[l]]).reshape(1, -1), LANES)
        x2d, h2, ri, rw, cnt = _merge(
            ya.reshape(t, D_MODEL), yb.reshape(t, MAMBA_INNER), proj, x2d,
            w_rwkv_o[l].astype(BF16), w_mamba_o[l].astype(BF16), w_out[l].astype(BF16),
            norm2_g[l].reshape(1, d), w_r, b_r, tm_merge)

        pos2, pends = _route_positions(ri, cnt, t // tm_moe, tm_moe)
        block_start = jnp.arange(n_blocks, dtype=jnp.int32) * EXPERT_ROWS
        block_e = jnp.minimum(jnp.sum(block_start[:, None] >= pends[None, :], axis=1), N_EXPERTS - 1).astype(jnp.int32)
        xb = _dispatch(pos2, h2, jnp.zeros((n_rows, D_MODEL), F32), tm_moe)
        ybuf = _expert_ffn(block_e, xb, w_e1[l].astype(BF16), w_e3[l].astype(BF16), w_e2[l].astype(BF16))
        x2d = _combine(pos2, ybuf, x2d, rw, final_g.reshape(1, d), l == depth - 1, tm_moe)
    return x2d.reshape(bsz, seq, d)
```

```python
import functools

import jax
import jax.numpy as jnp
from jax import lax
from jax.experimental import pallas as pl
from jax.experimental.pallas import tpu as pltpu

F32 = jnp.float32
BF16 = jnp.bfloat16

D_MODEL = 1024
RWKV_HEAD = 64
DECAY_LORA = 64
ICLR_LORA = 64
VALUE_LORA = 32
GATE_LORA = 128
RWKV_GN_EPS = 64e-5
MAMBA_INNER = 2048
MAMBA_HEAD = 64
MAMBA_HEADS = 32
MAMBA_GROUPS = 4
MAMBA_STATE = 128
MAMBA_CONV = 4
SSD_CHUNK = 128
N_EXPERT_GROUPS = 4
EXPERTS_PER_GROUP = 8
N_EXPERTS = 32
EXPERT_FF = 512
NORM_EPS = 1e-6
MAMBA_NORM_EPS = 1e-5

LANES = 128
SUBLANES = 8
D_TILES = D_MODEL // LANES
N_PAIRS = D_MODEL // LANES
WKV_CHUNK = 64
WKV_BATCH = 2
EXPERT_ROWS = 256

COL_Z = 0
COL_GATE = 2048
COL_XS = 4096
COL_R = 6144
COL_K = 7168
COL_V = 8192
COL_BC = 9216
COL_WA = 10240
COL_GLO = 10368
COL_DTV = 10496
N_COLS = 10752
VLO_LANE = 32
INPROJ_TN = N_COLS // 7

VMEM_LIMIT = 56 * 1024 * 1024


def _softplus(x):
    return jnp.maximum(x, 0.0) + jnp.log(1.0 + jnp.exp(-jnp.abs(x)))


def _sigmoid(x):
    return 1.0 / (1.0 + jnp.exp(-x))


def _dot(a, b):
    return jnp.dot(a, b, preferred_element_type=F32)


def _dot_nt(a, b):
    return lax.dot_general(a, b, (((1,), (1,)), ((), ())), preferred_element_type=F32)


def _split_hi_lo(x):
    hi = x.astype(BF16)
    lo = (x - hi.astype(F32)).astype(BF16)
    return hi, lo


def _inproj_kernel(x_ref, g_ref, w_ref, o_ref, h_scr):
    @pl.when(pl.program_id(1) == 0)
    def _():
        x = x_ref[...]
        ms = jnp.mean(x * x, axis=-1, keepdims=True)
        h_scr[...] = (x * lax.rsqrt(ms + NORM_EPS) * g_ref[...]).astype(BF16)

    o_ref[...] = _dot(h_scr[...], w_ref[...]).astype(o_ref.dtype)


def _inproj(x2d, g_row, w_bf16, tm, tn):
    t, d = x2d.shape
    nc = w_bf16.shape[1]
    return pl.pallas_call(
        _inproj_kernel,
        out_shape=jax.ShapeDtypeStruct((t, nc), BF16),
        grid=(t // tm, nc // tn),
        in_specs=[
            pl.BlockSpec((tm, d), lambda i, j: (i, 0)),
            pl.BlockSpec((1, d), lambda i, j: (0, 0)),
            pl.BlockSpec((d, tn), lambda i, j: (0, j)),
        ],
        out_specs=pl.BlockSpec((tm, tn), lambda i, j: (i, j)),
        scratch_shapes=[pltpu.VMEM((tm, d), BF16)],
        compiler_params=pltpu.CompilerParams(
            dimension_semantics=("arbitrary", "arbitrary"), vmem_limit_bytes=VMEM_LIMIT),
        name="inproj",
    )(x2d, g_row, w_bf16)


RP_MU_R, RP_MU_K, RP_MU_V, RP_W0, RP_A0, RP_V0, RP_KK, RP_KA, RP_RK, RP_LNG, RP_LNB = range(11)
RP_ROWS = 16
SP_MU_WA, SP_MU_G, SP_MU_VL = range(3)
LR_DECAY, LR_ICLR, LR_GATE, LR_VAL = range(4)


def _bd_stack(x, m0):
    zero = jnp.zeros_like(x)
    return jnp.concatenate([jnp.where(m0, x, zero), jnp.where(m0, zero, x)], axis=0)


def _rwkv_kernel(has_vres, *refs):
    if has_vres:
        (r_ref, k_ref, v_ref, wa_ref, gl_ref, dtv_ref, vf_ref, rowp_ref, smallp_ref, lora_ref,
         y_ref, c_rkv, c_small, state, s_r, s_k, s_v, s_lw, s_wc, s_a, s_g) = refs
        vout_ref = None
    else:
        (r_ref, k_ref, v_ref, wa_ref, gl_ref, rowp_ref, smallp_ref, lora_ref,
         y_ref, vout_ref, c_rkv, c_small, state, s_r, s_k, s_v, s_lw, s_wc, s_a, s_g) = refs
        dtv_ref = vf_ref = None
    L = WKV_CHUNK
    c = pl.program_id(1)
    first = c == 0
    row = lax.broadcasted_iota(jnp.int32, (L, LANES), 0)
    lane = lax.broadcasted_iota(jnp.int32, (L, LANES), 1)

    @pl.when(first)
    def _():
        state[...] = jnp.zeros_like(state)
        c_rkv[...] = jnp.zeros_like(c_rkv)
        c_small[...] = jnp.zeros_like(c_small)

    def shift_mix(src, bb, carry_ref, idx, mu):
        u = src[bb].astype(F32)
        row0 = lax.broadcasted_iota(jnp.int32, u.shape, 0) == 0
        prev = jnp.where(row0, carry_ref[bb, idx, 0:1], pltpu.roll(u, 1, 0))
        carry_ref[bb, idx, 0:1] = u[L - 1:L]
        return u + (prev - u) * mu

    rowf = lambda slot: rowp_ref[slot:slot + 1]
    tril = (lax.broadcasted_iota(jnp.int32, (L, L), 0)
            >= lax.broadcasted_iota(jnp.int32, (L, L), 1)).astype(BF16)
    for bb in range(WKV_BATCH):
        s_r[bb] = shift_mix(r_ref, bb, c_rkv, 0, rowf(RP_MU_R))
        s_k[bb] = shift_mix(k_ref, bb, c_rkv, 1, rowf(RP_MU_K))
        xv = shift_mix(v_ref, bb, c_rkv, 2, rowf(RP_MU_V))
        xwa = shift_mix(wa_ref, bb, c_small, 0, smallp_ref[SP_MU_WA:SP_MU_WA + 1])
        xg = shift_mix(gl_ref, bb, c_small, 1, smallp_ref[SP_MU_G:SP_MU_G + 1])
        if has_vres:
            xvl = shift_mix(dtv_ref, bb, c_small, 2, smallp_ref[SP_MU_VL:SP_MU_VL + 1])
            mix = _sigmoid(rowf(RP_V0) + _dot(xvl.astype(BF16), lora_ref[LR_VAL]))
            xv = xv + (vf_ref[bb].astype(F32) - xv) * mix
        else:
            vout_ref[bb] = xv.astype(vout_ref.dtype)
        s_v[bb] = xv
        w_in = rowf(RP_W0) + _dot(jnp.tanh(xwa).astype(BF16), lora_ref[LR_DECAY])
        lw_all = -jnp.exp(-_softplus(-w_in) - 0.5)
        s_lw[bb] = lw_all
        lw_hi, lw_lo = _split_hi_lo(lw_all)
        s_wc[bb] = _dot(tril, lw_hi) + _dot(tril, lw_lo)
        s_a[bb] = _sigmoid(rowf(RP_A0) + _dot(xwa.astype(BF16), lora_ref[LR_ICLR]))
        s_g[bb] = _dot(_sigmoid(xg).astype(BF16), lora_ref[LR_GATE])

    m0 = lane < RWKV_HEAD
    s_in = lane % RWKV_HEAD
    strict = s_in < row
    incl = s_in <= row
    lane2 = lax.broadcasted_iota(jnp.int32, (2 * L, LANES), 1)
    row2 = lax.broadcasted_iota(jnp.int32, (2 * L, LANES), 0)
    bd_mask = (lane2 // RWKV_HEAD) == (row2 // RWKV_HEAD)
    bd_ones = bd_mask.astype(BF16)
    lane4 = lax.broadcasted_iota(jnp.int32, (4 * L, 2 * LANES), 1)
    row4 = lax.broadcasted_iota(jnp.int32, (4 * L, 2 * LANES), 0)
    bd_ones2 = ((lane4 // RWKV_HEAD) == (row4 // RWKV_HEAD)).astype(BF16)
    HB = L // 2
    lane_h = lax.broadcasted_iota(jnp.int32, (HB, LANES), 1)
    row_h = lax.broadcasted_iota(jnp.int32, (HB, LANES), 0)
    second_half = (lane_h % RWKV_HEAD) >= HB
    m0_h = lane_h < RWKV_HEAD
    eye_h = ((lane_h % HB) == row_h).astype(F32)
    blk_h = lane_h // HB
    zeros_h = jnp.zeros((HB, LANES), F32)

    def bd4(xh):
        return jnp.concatenate([jnp.where(blk_h == b, xh, 0.0) for b in range(4)], axis=0).astype(BF16)

    def rows_for_half(xh, half):
        parts = [jnp.where(m0_h, xh, 0.0), zeros_h, jnp.where(m0_h, 0.0, xh), zeros_h]
        if half == 1:
            parts = [zeros_h, parts[0], zeros_h, parts[2]]
        return jnp.concatenate(parts, axis=0).astype(BF16)

    def pair_steps(bb, p):
        sl = slice(p * LANES, (p + 1) * LANES)
        rp = lambda slot: rowp_ref[slot:slot + 1, sl]
        r = s_r[bb, :, sl]
        xk = s_k[bb, :, sl]
        v = s_v[bb, :, sl]
        lw = s_lw[bb, :, sl]
        w_cum = s_wc[bb, :, sl]
        a = s_a[bb, :, sl]
        kkr = xk * rp(RP_KK)
        k2 = xk * (1.0 + (a - 1.0) * rp(RP_KA))
        sums = _dot(jnp.concatenate([kkr * kkr, r * k2 * rp(RP_RK)], axis=1).astype(BF16), bd_ones2)
        yield
        kkn = kkr / jnp.maximum(jnp.sqrt(sums[:, 0:LANES]), 1e-12)
        bonus_s = sums[:, LANES:2 * LANES]
        avec = -kkn
        bvec = kkn * a
        w_mid = w_cum[L // 2 - 1:L // 2]
        w_end = w_cum[L - 1:L]
        e_abs = jnp.exp(w_cum)
        e_prev = jnp.exp(w_cum - lw)
        e_from_mid = jnp.exp(w_mid - w_cum)
        e_to_end = jnp.exp(w_end - w_cum)
        e_mid_inv = jnp.exp(-w_mid)
        r_abs = r * e_abs
        a_abs = avec * e_prev
        r_mid = r_abs * e_mid_inv
        a_mid = a_abs * e_mid_inv
        b_mid = bvec * e_from_mid
        k_mid = k2 * e_from_mid
        b_end = bvec * e_to_end
        k_end = k2 * e_to_end

        st = state[bb * N_PAIRS + p]
        lhs_abs = jnp.concatenate([a_abs, r_abs], axis=0).astype(BF16)
        m1 = _dot_nt(lhs_abs, st.astype(BF16))
        lhs_mid = jnp.concatenate([a_mid, r_mid], axis=0).astype(BF16)
        rhs_mid = jnp.concatenate([_bd_stack(b_mid, m0), _bd_stack(k_mid, m0)], axis=0).astype(BF16)
        m2 = _dot_nt(lhs_mid, rhs_mid)
        yield
        a_ab = jnp.where(strict, m2[0:L, 0:LANES], 0.0)
        a_ak = jnp.where(strict, m2[0:L, LANES:2 * LANES], 0.0)
        a_rb = jnp.where(incl, m2[L:2 * L, 0:LANES], 0.0)
        a_rk = jnp.where(incl, m2[L:2 * L, LANES:2 * LANES], 0.0)

        v_bd = _bd_stack(v, m0).astype(BF16)
        rhs_u = m1[0:L] + _dot(a_ak.astype(BF16), v_bd)

        top = a_ab[0:HB]
        bot = a_ab[HB:L]
        diag = jnp.where(second_half, bot, top)
        a21 = jnp.where(second_half, 0.0, bot)
        x_inv = eye_h + diag
        pw = _dot(diag.astype(BF16), bd4(diag))
        yield
        for _ in range(3):
            res = _dot(jnp.concatenate([x_inv, pw], axis=0).astype(BF16), bd4(pw))
            yield
            x_inv = x_inv + res[0:HB]
            pw = res[HB:L]
        x_upd = _dot(x_inv.astype(BF16), bd4(pw))
        yield
        x_inv_bf = (x_inv + x_upd).astype(BF16)
        u1 = _dot(x_inv_bf, rows_for_half(rhs_u[0:HB], 0))
        yield
        rhs2 = rhs_u[HB:L] + _dot(a21.astype(BF16), rows_for_half(u1, 0))
        yield
        u2 = _dot(x_inv_bf, rows_for_half(rhs2, 1))
        yield
        u = jnp.concatenate([u1, u2], axis=0)
        u_bd = _bd_stack(u, m0).astype(BF16)
        y = m1[L:2 * L] + _dot(jnp.concatenate([a_rb, a_rk], axis=1).astype(BF16),
                               jnp.concatenate([u_bd, v_bd], axis=0))

        uv_t = jnp.concatenate([u, v], axis=0).T.astype(BF16)
        upd = _dot(uv_t, jnp.concatenate([b_end, k_end], axis=0).astype(BF16))
        yield
        state[bb * N_PAIRS + p] = st * jnp.exp(w_end) + jnp.where(bd_mask, upd, 0.0)

        inv_n = 1.0 / RWKV_HEAD
        mean = _dot(y.astype(BF16), bd_ones) * inv_n
        yield
        yc = y - mean
        var = _dot((yc * yc).astype(BF16), bd_ones) * inv_n
        yield
        yn = yc * lax.rsqrt(var + RWKV_GN_EPS) * rp(RP_LNG) + rp(RP_LNB)
        y_ref[bb, :, sl] = ((yn + bonus_s * v) * s_g[bb, :, sl]).astype(y_ref.dtype)

    chains = [pair_steps(bb, p) for bb in range(WKV_BATCH) for p in range(N_PAIRS)]
    while chains:
        alive = []
        for chain in chains:
            try:
                next(chain)
                alive.append(chain)
            except StopIteration:
                pass
        chains = alive


def _rwkv_branch(proj3, v_first, rowp, smallp, lora):
    b, s, _ = proj3.shape
    L = WKV_CHUNK
    has_vres = v_first is not None
    nb = WKV_BATCH
    wide = lambda col: pl.BlockSpec((nb, L, D_MODEL), lambda i, j, c=col // D_MODEL: (i, j, c))
    narrow = lambda col: pl.BlockSpec((nb, L, LANES), lambda i, j, c=col // LANES: (i, j, c))
    full = lambda shape: pl.BlockSpec(shape, lambda i, j: (0,) * len(shape))
    out_blk = pl.BlockSpec((nb, L, D_MODEL), lambda i, j: (i, j, 0))
    in_specs = [wide(COL_R), wide(COL_K), wide(COL_V), narrow(COL_WA), narrow(COL_GLO)]
    args = [proj3] * 5
    if has_vres:
        in_specs += [narrow(COL_DTV), out_blk]
        args += [proj3, v_first]
    in_specs += [full(rowp.shape), full(smallp.shape), full(lora.shape)]
    args += [rowp, smallp, lora]
    wide_scr = lambda: pltpu.VMEM((nb, L, D_MODEL), F32)
    scratch = [pltpu.VMEM((nb, 3, SUBLANES, D_MODEL), F32), pltpu.VMEM((nb, 3, SUBLANES, LANES), F32),
               pltpu.VMEM((nb * N_PAIRS, 2 * RWKV_HEAD, LANES), F32)] + [wide_scr() for _ in range(7)]
    y_shape = jax.ShapeDtypeStruct((b, s, D_MODEL), BF16)
    out_shape = y_shape if has_vres else (y_shape, y_shape)
    out_specs = out_blk if has_vres else (out_blk, out_blk)
    return pl.pallas_call(
        functools.partial(_rwkv_kernel, has_vres),
        out_shape=out_shape,
        grid=(b // nb, s // L),
        in_specs=in_specs,
        out_specs=out_specs,
        scratch_shapes=scratch,
        compiler_params=pltpu.CompilerParams(
            dimension_semantics=("arbitrary", "arbitrary"), vmem_limit_bytes=VMEM_LIMIT),
        name="rwkv7",
    )(*args)


HEADS_PER_GROUP = MAMBA_HEADS // MAMBA_GROUPS
GROUP_WIDTH = MAMBA_INNER // MAMBA_GROUPS
BC_WIDTH = 2 * MAMBA_GROUPS * MAMBA_STATE
DP_DT_BIAS, DP_A_HEAD = range(2)
LOG2_E = 1.4426950408889634


def _silu(x):
    return x * _sigmoid(x)


def _mamba_kernel(z_ref, xs_ref, bc_ref, dtv_ref, cw_xs_ref, cb_xs_ref, cw_bc_ref, cb_bc_ref, dtp_ref,
                  dskip_ref, mnorm_ref, expand_ref, y_ref, tail_xs, tail_bc, state):
    L = SSD_CHUNK
    c = pl.program_id(1)
    first = c == 0

    @pl.when(first)
    def _():
        state[...] = jnp.zeros_like(state)
        tail_xs[...] = jnp.zeros_like(tail_xs)
        tail_bc[...] = jnp.zeros_like(tail_bc)

    def conv_silu(src_ref, tail_ref, w_ref, b_ref, cols):
        u = src_ref[0, :, cols].astype(F32)
        row8 = lax.broadcasted_iota(jnp.int32, (SUBLANES, u.shape[1]), 0)
        tail = tail_ref[:, cols]
        acc = u * w_ref[MAMBA_CONV - 1:MAMBA_CONV, cols] + b_ref[:, cols]
        for d in range(1, MAMBA_CONV):
            rolled = pltpu.roll(u, d, 0)
            head = jnp.where(row8 < d, pltpu.roll(tail, d, 0), rolled[0:SUBLANES])
            shifted = jnp.concatenate([head, rolled[SUBLANES:]], axis=0)
            acc = acc + shifted * w_ref[MAMBA_CONV - 1 - d:MAMBA_CONV - d, cols]
        tail_ref[:, cols] = u[L - SUBLANES:L]
        return _silu(acc)

    dt = _softplus(dtv_ref[0].astype(F32) + dtp_ref[DP_DT_BIAS:DP_DT_BIAS + 1])
    adt = dt * dtp_ref[DP_A_HEAD:DP_A_HEAD + 1]
    tril_b = lax.broadcasted_iota(jnp.int32, (L, L), 0) >= lax.broadcasted_iota(jnp.int32, (L, L), 1)
    tril = tril_b.astype(BF16)
    adt_hi, adt_lo = _split_hi_lo(adt)
    a_cs = _dot(tril, adt_hi) + _dot(tril, adt_lo)
    a_cs2 = a_cs * LOG2_E
    a_cs2_t = a_cs2.T
    dt_bf = dt.astype(BF16)
    exp_acs_bf = jnp.exp(a_cs).astype(BF16)
    to_end_bf = jnp.exp(a_cs[L - 1:L] - a_cs).astype(BF16)
    lane = lax.broadcasted_iota(jnp.int32, (L, LANES), 1)
    m0 = lane < MAMBA_HEAD

    for g in range(MAMBA_GROUPS):
        gs = slice(g * GROUP_WIDTH, (g + 1) * GROUP_WIDTH)
        b_cols = slice(g * MAMBA_STATE, (g + 1) * MAMBA_STATE)
        c_cols = slice((MAMBA_GROUPS + g) * MAMBA_STATE, (MAMBA_GROUPS + g + 1) * MAMBA_STATE)
        xc = conv_silu(xs_ref, tail_xs, cw_xs_ref, cb_xs_ref, gs)
        bg = conv_silu(bc_ref, tail_bc, cw_bc_ref, cb_bc_ref, b_cols)
        cg_bf = conv_silu(bc_ref, tail_bc, cw_bc_ref, cb_bc_ref, c_cols).astype(BF16)
        expand = expand_ref[:, gs]
        exp_acs_x = _dot(exp_acs_bf, expand)
        xdt = xc * _dot(dt_bf, expand)
        xdt_bf = xdt.astype(BF16)
        xw_bf = (xdt * _dot(to_end_bf, expand)).astype(BF16)
        cb = jnp.where(tril_b, _dot_nt(cg_bf, bg.astype(BF16)), 0.0)
        st = state[g]
        y_off = _dot(cg_bf, st.astype(BF16)) * exp_acs_x
        y_parts = []
        for jp in range(HEADS_PER_GROUP // 2):
            ws = []
            for hh in range(2):
                j = g * HEADS_PER_GROUP + 2 * jp + hh
                seg2 = a_cs2[:, j:j + 1] - a_cs2_t[j:j + 1, :]
                ws.append((cb * jnp.exp2(jnp.minimum(seg2, 0.0))).astype(BF16))
            ps = slice(jp * LANES, (jp + 1) * LANES)
            y_parts.append(_dot(jnp.concatenate(ws, axis=1), _bd_stack(xdt_bf[:, ps], m0)))
        state[g] = st * exp_acs_x[L - 1:L] + _dot(bg.T.astype(BF16), xw_bf)
        y = jnp.concatenate(y_parts, axis=1) + y_off + xc * dskip_ref[:, gs]
        y = y * _silu(z_ref[0, :, gs].astype(F32))
        ms = jnp.mean(y * y, axis=-1, keepdims=True)
        y_ref[0, :, gs] = (y * lax.rsqrt(ms + MAMBA_NORM_EPS) * mnorm_ref[:, gs]).astype(y_ref.dtype)


def _mamba_branch(proj3, cw_xs, cb_xs, cw_bc, cb_bc, dtp, dskip_x, mnorm, expand):
    b, s, _ = proj3.shape
    L = SSD_CHUNK
    blk = lambda width, col: pl.BlockSpec((1, L, width), lambda i, j, c=col // width: (i, j, c))
    full = lambda arr: pl.BlockSpec(arr.shape, lambda i, j: (0,) * arr.ndim)
    params = [cw_xs, cb_xs, cw_bc, cb_bc, dtp, dskip_x, mnorm, expand]
    return pl.pallas_call(
        _mamba_kernel,
        out_shape=jax.ShapeDtypeStruct((b, s, MAMBA_INNER), BF16),
        grid=(b, s // L),
        in_specs=[blk(MAMBA_INNER, COL_Z), blk(MAMBA_INNER, COL_XS), blk(BC_WIDTH, COL_BC),
                  blk(LANES, COL_DTV)] + [full(a) for a in params],
        out_specs=pl.BlockSpec((1, L, MAMBA_INNER), lambda i, j: (i, j, 0)),
        scratch_shapes=[pltpu.VMEM((SUBLANES, MAMBA_INNER), F32), pltpu.VMEM((SUBLANES, BC_WIDTH), F32),
                        pltpu.VMEM((MAMBA_GROUPS, MAMBA_STATE, GROUP_WIDTH), F32)],
        compiler_params=pltpu.CompilerParams(
            dimension_semantics=("arbitrary", "arbitrary"), vmem_limit_bytes=VMEM_LIMIT),
        name="mamba2_ssd",
    )(proj3, proj3, proj3, proj3, *params)


ROUTE_E0 = N_EXPERT_GROUPS
RI_E1, RI_E2, RI_RANK1, RI_RANK2 = range(4)
BIG = 1e30


def _merge_kernel(ya_ref, yb_ref, gate_ref, x_ref, wro_ref, wmo_ref, wout_ref, g2_ref, wr_ref, br_ref,
                  xo_ref, h2_ref, ri_ref, rw_ref, cnt_ref, run_cnt):
    tm = x_ref.shape[0]
    i = pl.program_id(0)

    @pl.when(i == 0)
    def _():
        run_cnt[...] = jnp.zeros_like(run_cnt)

    o_a = _dot(ya_ref[...], wro_ref[...])
    o_b = _dot(yb_ref[...], wmo_ref[...])
    gates = _sigmoid(gate_ref[...].astype(F32))
    merged = gates[:, 0:D_MODEL] * o_a + gates[:, D_MODEL:2 * D_MODEL] * o_b
    x = x_ref[...] + _dot(merged.astype(BF16), wout_ref[...])
    xo_ref[...] = x
    ms = jnp.mean(x * x, axis=-1, keepdims=True)
    h2 = x * lax.rsqrt(ms + NORM_EPS) * g2_ref[...]
    h2_ref[...] = h2

    logits = _dot(h2.astype(BF16), wr_ref[...]) + br_ref[...]
    lane = lax.broadcasted_iota(jnp.int32, (tm, LANES), 1)
    gmask = lane < N_EXPERT_GROUPS
    gl = jnp.where(gmask, logits, -BIG)
    gmax = jnp.max(gl, axis=-1, keepdims=True)
    gidx = jnp.min(jnp.where(gl == gmax, lane, LANES), axis=-1, keepdims=True)
    gsum = jnp.sum(jnp.where(gmask, jnp.exp(gl - gmax), 0.0), axis=-1, keepdims=True)
    g_p = 1.0 / gsum
    e_lo = ROUTE_E0 + gidx * EXPERTS_PER_GROUP
    emask = (lane >= e_lo) & (lane < e_lo + EXPERTS_PER_GROUP)
    el = jnp.where(emask, logits, -BIG)
    e1 = jnp.max(el, axis=-1, keepdims=True)
    i1 = jnp.min(jnp.where(el == e1, lane, LANES), axis=-1, keepdims=True)
    el2 = jnp.where(lane == i1, -BIG, el)
    e2 = jnp.max(el2, axis=-1, keepdims=True)
    i2 = jnp.min(jnp.where(el2 == e2, lane, LANES), axis=-1, keepdims=True)
    q = jnp.exp(e2 - e1)
    w1 = g_p / (1.0 + q)
    w2 = g_p * q / (1.0 + q)
    eid1 = i1 - ROUTE_E0
    eid2 = i2 - ROUTE_E0

    onehot = ((lane == eid1) | (lane == eid2)).astype(BF16)
    strict = (lax.broadcasted_iota(jnp.int32, (tm, tm), 0)
              > lax.broadcasted_iota(jnp.int32, (tm, tm), 1)).astype(BF16)
    before = _dot(strict, onehot) + run_cnt[0:1]
    rank1 = jnp.sum(jnp.where(lane == eid1, before, 0.0), axis=-1, keepdims=True).astype(jnp.int32)
    rank2 = jnp.sum(jnp.where(lane == eid2, before, 0.0), axis=-1, keepdims=True).astype(jnp.int32)
    total = run_cnt[0:1] + jnp.sum(onehot.astype(F32), axis=0, keepdims=True)
    run_cnt[0:1] = total
    cnt_ref[...] = jnp.broadcast_to(total, cnt_ref.shape)

    ri = jnp.where(lane == RI_E1, eid1, 0)
    ri = jnp.where(lane == RI_E2, eid2, ri)
    ri = jnp.where(lane == RI_RANK1, rank1, ri)
    ri = jnp.where(lane == RI_RANK2, rank2, ri)
    ri_ref[...] = ri
    rw_ref[...] = jnp.where(lane == 0, w1, jnp.where(lane == 1, w2, 0.0))


def _merge(ya, yb, proj, x2d, w_ro, w_mo, w_out, g2_row, w_r, b_r, tm):
    t = x2d.shape[0]
    full = lambda arr: pl.BlockSpec(arr.shape, lambda i: (0,) * arr.ndim, pipeline_mode=pl.Buffered(1))
    rows = lambda width: pl.BlockSpec((tm, width), lambda i: (i, 0))
    return pl.pallas_call(
        _merge_kernel,
        out_shape=(jax.ShapeDtypeStruct((t, D_MODEL), F32),
                   jax.ShapeDtypeStruct((t, D_MODEL), F32),
                   jax.ShapeDtypeStruct((t, LANES), jnp.int32),
                   jax.ShapeDtypeStruct((t, LANES), F32),
                   jax.ShapeDtypeStruct((SUBLANES, LANES), F32)),
        grid=(t // tm,),
        in_specs=[rows(D_MODEL), rows(MAMBA_INNER),
                  pl.BlockSpec((tm, 2 * D_MODEL), lambda i: (i, COL_GATE // (2 * D_MODEL))),
                  rows(D_MODEL), full(w_ro), full(w_mo), full(w_out), full(g2_row), full(w_r), full(b_r)],
        out_specs=(rows(D_MODEL), rows(D_MODEL),
                   rows(LANES), rows(LANES), pl.BlockSpec((SUBLANES, LANES), lambda i: (0, 0))),
        scratch_shapes=[pltpu.VMEM((SUBLANES, LANES), F32)],
        compiler_params=pltpu.CompilerParams(
            dimension_semantics=("arbitrary",), vmem_limit_bytes=VMEM_LIMIT),
        name="merge_route",
    )(ya, yb, proj, x2d, w_ro, w_mo, w_out, g2_row, w_r, b_r)


ROW_UNROLL = 8


def _fetch_step_indices(pos_hbm, idx_smem, idx_sem):
    i = pl.program_id(0)
    n = pl.num_programs(0)
    slot = i % 2
    per_step = pos_hbm.shape[1]

    def copy(step, s):
        dst = idx_smem.at[pl.ds(pl.multiple_of(s * per_step, per_step), per_step)]
        return pltpu.make_async_copy(pos_hbm.at[step], dst, idx_sem.at[s])

    @pl.when(i == 0)
    def _():
        copy(0, 0).start()

    copy(i, slot).wait()

    @pl.when(i + 1 < n)
    def _():
        copy(i + 1, 1 - slot).start()

    return slot * per_step


def _issue_row_copies(row_copy, n_groups):
    def issue(g, carry):
        for u in range(SUBLANES):
            for k in range(2):
                row_copy(g, u, k).start(priority=k)
        return carry

    lax.fori_loop(0, n_groups, issue, 0)


META_PEND, META_PCOUNT, META_N_USED = 0, N_EXPERTS, 2 * N_EXPERTS
META_SIZE = 2 * N_EXPERTS + 1


def _dispatch_kernel(min_used_blocks, meta_ref, pos_hbm, h2_ref, xb_out, idx_smem, zero_blk, idx_sem, row_sem,
                     zero_sem):
    n_groups = h2_ref.shape[0]
    tm = n_groups * SUBLANES
    n_blocks = xb_out.shape[0] // EXPERT_ROWS
    n_used = meta_ref[META_N_USED]

    @pl.when(pl.program_id(0) == 0)
    def _():
        zero_blk[...] = jnp.zeros_like(zero_blk)

        def zero_fill(start):
            dst = xb_out.at[pl.ds(pl.multiple_of(start, EXPERT_ROWS), EXPERT_ROWS)]
            return pltpu.make_async_copy(zero_blk, dst, zero_sem)

        fills = [(meta_ref[META_PCOUNT + e] > 0, meta_ref[META_PEND + e] - EXPERT_ROWS)
                 for e in range(N_EXPERTS)]
        fills += [(blk >= n_used, blk * EXPERT_ROWS) for blk in range(min_used_blocks, n_blocks)]
        for pred, start in fills:
            @pl.when(pred)
            def _():
                zero_fill(start).start()
        for pred, start in fills:
            @pl.when(pred)
            def _():
                zero_fill(start).wait()

    base = _fetch_step_indices(pos_hbm, idx_smem, idx_sem)

    def row_copy(g, u, k):
        row = idx_smem[base + 2 * SUBLANES * g + (2 * u + k)]
        return pltpu.make_async_copy(h2_ref.at[g, pl.ds(u, 1)], xb_out.at[pl.ds(row, 1)], row_sem)

    _issue_row_copies(row_copy, n_groups)
    for _ in range(2):
        pltpu.make_async_copy(xb_out.at[pl.ds(0, tm)], xb_out.at[pl.ds(0, tm)], row_sem).wait()


def _dispatch(meta, pos2, h2, n_rows, tm):
    t = h2.shape[0]
    min_used_blocks = (2 * t) // EXPERT_ROWS
    grid_spec = pltpu.PrefetchScalarGridSpec(
        num_scalar_prefetch=1,
        grid=(t // tm,),
        in_specs=[pl.BlockSpec(memory_space=pl.ANY),
                  pl.BlockSpec((tm // SUBLANES, SUBLANES, D_MODEL), lambda i, meta: (i, 0, 0))],
        out_specs=pl.BlockSpec(memory_space=pl.ANY),
        scratch_shapes=[pltpu.SMEM((4 * tm,), jnp.int32), pltpu.VMEM((EXPERT_ROWS, D_MODEL), F32),
                        pltpu.SemaphoreType.DMA((2,)), pltpu.SemaphoreType.DMA, pltpu.SemaphoreType.DMA],
    )
    return pl.pallas_call(
        functools.partial(_dispatch_kernel, min_used_blocks),
        out_shape=jax.ShapeDtypeStruct((n_rows, D_MODEL), F32),
        grid_spec=grid_spec,
        compiler_params=pltpu.CompilerParams(
            dimension_semantics=("arbitrary",), vmem_limit_bytes=VMEM_LIMIT),
        name="moe_dispatch",
    )(meta, pos2, h2.reshape(t // SUBLANES, SUBLANES, D_MODEL))


def _ffn_kernel(be_ref, xb_ref, w1_ref, w3_ref, w2_ref, yb_ref):
    n_used = be_ref[pl.num_programs(0)]

    @pl.when(pl.program_id(0) < n_used)
    def _():
        x = xb_ref[...].astype(BF16)
        h1 = _dot(x, w1_ref[0])
        h3 = _dot(x, w3_ref[0])
        hid = (_silu(h1) * h3).astype(BF16)
        yb_ref[...] = _dot(hid, w2_ref[0])

    @pl.when(pl.program_id(0) >= n_used)
    def _():
        yb_ref[...] = jnp.zeros_like(yb_ref)


def _expert_ffn(block_e, xb, w1, w3, w2):
    n_rows = xb.shape[0]
    br = EXPERT_ROWS
    grid_spec = pltpu.PrefetchScalarGridSpec(
        num_scalar_prefetch=1,
        grid=(n_rows // br,),
        in_specs=[pl.BlockSpec((br, D_MODEL), lambda i, be: (i, 0)),
                  pl.BlockSpec((1, D_MODEL, EXPERT_FF), lambda i, be: (be[i], 0, 0)),
                  pl.BlockSpec((1, D_MODEL, EXPERT_FF), lambda i, be: (be[i], 0, 0)),
                  pl.BlockSpec((1, EXPERT_FF, D_MODEL), lambda i, be: (be[i], 0, 0))],
        out_specs=pl.BlockSpec((br, D_MODEL), lambda i, be: (i, 0)),
    )
    return pl.pallas_call(
        _ffn_kernel,
        out_shape=jax.ShapeDtypeStruct(xb.shape, F32),
        grid_spec=grid_spec,
        compiler_params=pltpu.CompilerParams(
            dimension_semantics=("arbitrary",), vmem_limit_bytes=VMEM_LIMIT),
        name="expert_ffn",
    )(block_e, xb, w1, w3, w2)


def _combine_kernel(final_norm, pos_hbm, yb_hbm, x_ref, rw_ref, fg_ref, o_ref, idx_smem, buf, idx_sem, row_sem):
    tm = x_ref.shape[0]
    n_groups = tm // SUBLANES
    base = _fetch_step_indices(pos_hbm, idx_smem, idx_sem)

    def row_copy(g, u, k):
        row = idx_smem[base + 2 * SUBLANES * g + (2 * u + k)]
        return pltpu.make_async_copy(yb_hbm.at[pl.ds(row, 1)], buf.at[k, g, pl.ds(u, 1)], row_sem)

    _issue_row_copies(row_copy, n_groups)
    for _ in range(2):
        pltpu.make_async_copy(yb_hbm.at[pl.ds(0, tm)], yb_hbm.at[pl.ds(0, tm)], row_sem).wait()

    rw = rw_ref[...]
    y1 = buf[0].reshape(tm, D_MODEL)
    y2 = buf[1].reshape(tm, D_MODEL)
    x = x_ref[...] + rw[:, 0:1] * y1 + rw[:, 1:2] * y2
    if final_norm:
        ms = jnp.mean(x * x, axis=-1, keepdims=True)
        x = x * lax.rsqrt(ms + NORM_EPS) * fg_ref[...]
    o_ref[...] = x


def _combine(pos2, yb, x2d, rw, final_g_row, final_norm, tm):
    t = x2d.shape[0]
    return pl.pallas_call(
        functools.partial(_combine_kernel, final_norm),
        out_shape=jax.ShapeDtypeStruct(x2d.shape, F32),
        grid=(t // tm,),
        in_specs=[pl.BlockSpec(memory_space=pl.ANY), pl.BlockSpec(memory_space=pl.ANY),
                  pl.BlockSpec((tm, D_MODEL), lambda i: (i, 0)),
                  pl.BlockSpec((tm, LANES), lambda i: (i, 0)),
                  pl.BlockSpec((1, D_MODEL), lambda i: (0, 0))],
        out_specs=pl.BlockSpec((tm, D_MODEL), lambda i: (i, 0)),
        scratch_shapes=[pltpu.SMEM((4 * tm,), jnp.int32),
                        pltpu.VMEM((2, tm // SUBLANES, SUBLANES, D_MODEL), F32),
                        pltpu.SemaphoreType.DMA((2,)), pltpu.SemaphoreType.DMA],
        compiler_params=pltpu.CompilerParams(
            dimension_semantics=("arbitrary",), vmem_limit_bytes=VMEM_LIMIT),
        name="moe_combine",
    )(pos2, yb, x2d, rw, final_g_row)


def _pad_cols(w, width):
    return jnp.pad(w, ((0, 0), (0, width - w.shape[1])))


def _pack_w_in(w_in_l, w_v1_l):
    rw = 3 * D_MODEL
    o_wlo = rw
    o_alo = o_wlo + DECAY_LORA
    o_glo = o_alo + ICLR_LORA
    o_m = o_glo + GATE_LORA
    o_xbc = o_m + MAMBA_INNER
    o_dt = o_xbc + MAMBA_INNER + BC_WIDTH
    o_gate = o_dt + MAMBA_HEADS
    d = w_in_l.shape[0]
    vlo = w_v1_l if w_v1_l is not None else jnp.zeros((d, VALUE_LORA), w_in_l.dtype)
    dtv = _pad_cols(jnp.concatenate([w_in_l[:, o_dt:o_dt + MAMBA_HEADS], vlo], axis=1), LANES)
    cols = [
        w_in_l[:, o_m:o_m + MAMBA_INNER],
        w_in_l[:, o_gate:o_gate + 2 * D_MODEL],
        w_in_l[:, o_xbc:o_xbc + MAMBA_INNER],
        w_in_l[:, 0:rw],
        w_in_l[:, o_xbc + MAMBA_INNER:o_xbc + MAMBA_INNER + BC_WIDTH],
        w_in_l[:, o_wlo:o_glo],
        w_in_l[:, o_glo:o_m],
        dtv,
        jnp.zeros((d, N_COLS - COL_DTV - LANES), w_in_l.dtype),
    ]
    return jnp.concatenate(cols, axis=1).astype(BF16)


def _pack_rwkv_params(l, mu_rwkv, w0, a0, v0, k_k, k_a, r_k, lnx_g, lnx_b, w_decay2, w_iclr2, w_gate2,
                      mu_v, w_v2):
    mu = mu_rwkv[l]
    has_vres = l > 0
    zeros = jnp.zeros((D_MODEL,), F32)
    rows = [mu[0:D_MODEL], mu[D_MODEL:2 * D_MODEL], mu[2 * D_MODEL:3 * D_MODEL], w0[l], a0[l],
            v0[l - 1] if has_vres else zeros, k_k[l], k_a[l], r_k[l].reshape(-1), lnx_g[l], lnx_b[l]]
    rows += [zeros] * (RP_ROWS - len(rows))
    rowp = jnp.stack(rows, axis=0)
    rw = 3 * D_MODEL
    mu_vl = jnp.zeros((LANES,), F32)
    if has_vres:
        mu_vl = mu_vl.at[VLO_LANE:VLO_LANE + VALUE_LORA].set(mu_v[l - 1])
    smallp = jnp.stack([mu[rw:rw + LANES], mu[rw + LANES:rw + 2 * LANES], mu_vl]
                       + [jnp.zeros((LANES,), F32)] * (SUBLANES - 3), axis=0)
    z = lambda n: jnp.zeros((n, D_MODEL), F32)
    wd = jnp.concatenate([w_decay2[l], z(LANES - DECAY_LORA)], axis=0)
    wi = jnp.concatenate([z(DECAY_LORA), w_iclr2[l]], axis=0)
    wg = w_gate2[l]
    if has_vres:
        wv = jnp.concatenate([z(VLO_LANE), w_v2[l - 1], z(LANES - VLO_LANE - VALUE_LORA)], axis=0)
    else:
        wv = z(LANES)
    lora = jnp.stack([wd, wi, wg, wv], axis=0)
    return rowp, smallp, lora.astype(BF16)


def _pack_mamba_params(l, conv_w, conv_b, dt_bias, a_log, d_skip, mnorm_g):
    cw = conv_w[l].astype(F32)
    cb = conv_b[l].reshape(1, -1)
    cw_xs = jnp.concatenate([cw[:, :MAMBA_INNER], jnp.zeros((SUBLANES - MAMBA_CONV, MAMBA_INNER), F32)], axis=0)
    cw_bc = jnp.concatenate([cw[:, MAMBA_INNER:], jnp.zeros((SUBLANES - MAMBA_CONV, BC_WIDTH), F32)], axis=0)
    pad = lambda v: jnp.pad(v, (0, LANES - MAMBA_HEADS))
    dtp = jnp.stack([pad(dt_bias[l]), pad(-jnp.exp(a_log[l].astype(F32)))]
                    + [jnp.zeros((LANES,), F32)] * (SUBLANES - 2), axis=0)
    dskip_x = jnp.repeat(d_skip[l], MAMBA_HEAD).reshape(1, MAMBA_INNER)
    expand = (jnp.arange(LANES)[:, None] == (jnp.arange(MAMBA_INNER)[None, :] // MAMBA_HEAD)).astype(BF16)
    return cw_xs, cb[:, :MAMBA_INNER], cw_bc, cb[:, MAMBA_INNER:], dtp, dskip_x, mnorm_g[l].reshape(1, -1), expand


def _route_positions(ri, cnt, n_tiles, tile, n_blocks):
    counts = cnt[0, :N_EXPERTS].astype(jnp.int32)
    pcounts = (counts + EXPERT_ROWS - 1) // EXPERT_ROWS * EXPERT_ROWS
    pends = jnp.cumsum(pcounts)
    poffsets = pends - pcounts
    eid = ri[:, RI_E1:RI_E2 + 1]
    onehot = eid[:, :, None] == jnp.arange(N_EXPERTS, dtype=jnp.int32)
    pos = jnp.sum(jnp.where(onehot, poffsets, 0), axis=-1) + ri[:, RI_RANK1:RI_RANK2 + 1]
    n_used = pends[N_EXPERTS - 1:] // EXPERT_ROWS
    meta = jnp.concatenate([pends, pcounts, n_used]).astype(jnp.int32)
    block_start = jnp.arange(n_blocks, dtype=jnp.int32) * EXPERT_ROWS
    block_e = jnp.minimum(jnp.sum(block_start[:, None] >= pends[None, :], axis=1), N_EXPERTS - 1)
    block_e = jnp.concatenate([block_e.astype(jnp.int32), n_used.astype(jnp.int32)])
    return pos.reshape(n_tiles, 2 * tile), meta, block_e


def kernel(x, norm1_g, w_in, mu_rwkv, w0, w_decay2, a0, w_iclr2, w_gate2, k_k, k_a, r_k, lnx_g, lnx_b, w_rwkv_o, w_v1, mu_v, v0, w_v2, conv_w, conv_b, dt_bias, a_log, d_skip, mnorm_g, w_mamba_o, w_out, norm2_g, w_rg, b_rg, w_re, b_re, w_e1, w_e3, w_e2, final_g):
    bsz, seq, d = x.shape
    t = bsz * seq
    depth = w_in.shape[0]
    n_assign = 2 * t
    n_blocks = -(-(n_assign + N_EXPERTS * (EXPERT_ROWS - 1)) // EXPERT_ROWS)
    n_rows = n_blocks * EXPERT_ROWS
    tm_proj = min(1024, t)
    tm_merge = min(512, t)
    tm_moe = min(512, t)

    x2d = x.reshape(t, d)
    v_first = None
    for l in range(depth):
        w_cat = _pack_w_in(w_in[l], w_v1[l - 1] if l > 0 else None)
        proj = _inproj(x2d, norm1_g[l].reshape(1, d), w_cat, tm_proj, INPROJ_TN)
        proj3 = proj.reshape(bsz, seq, N_COLS)
        rowp, smallp, lora = _pack_rwkv_params(l, mu_rwkv, w0, a0, v0, k_k, k_a, r_k, lnx_g, lnx_b,
                                               w_decay2, w_iclr2, w_gate2, mu_v, w_v2)
        if l == 0:
            ya, v_first = _rwkv_branch(proj3, None, rowp, smallp, lora)
        else:
            ya = _rwkv_branch(proj3, v_first, rowp, smallp, lora)
        yb = _mamba_branch(proj3, *_pack_mamba_params(l, conv_w, conv_b, dt_bias, a_log, d_skip, mnorm_g))

        w_r = _pad_cols(jnp.concatenate([w_rg[l], w_re[l]], axis=1), LANES).astype(BF16)
        b_r = _pad_cols(jnp.concatenate([b_rg[l], b_re[l]]).reshape(1, -1), LANES)
        x2d, h2, ri, rw, cnt = _merge(
            ya.reshape(t, D_MODEL), yb.reshape(t, MAMBA_INNER), proj, x2d,
            w_rwkv_o[l].astype(BF16), w_mamba_o[l].astype(BF16), w_out[l].astype(BF16),
            norm2_g[l].reshape(1, d), w_r, b_r, tm_merge)

        pos2, meta, block_e = _route_positions(ri, cnt, t // tm_moe, tm_moe, n_blocks)
        xb = _dispatch(meta, pos2, h2, n_rows, tm_moe)
        ybuf = _expert_ffn(block_e, xb, w_e1[l].astype(BF16), w_e3[l].astype(BF16), w_e2[l].astype(BF16))
        x2d = _combine(pos2, ybuf, x2d, rw, final_g.reshape(1, d), l == depth - 1, tm_moe)
    return x2d.reshape(bsz, seq, d)
```

```python
import functools

import jax
import jax.numpy as jnp
from jax import lax
from jax.experimental import pallas as pl
from jax.experimental.pallas import tpu as pltpu

F32 = jnp.float32
BF16 = jnp.bfloat16

D_MODEL = 1024
RWKV_HEAD = 64
DECAY_LORA = 64
ICLR_LORA = 64
VALUE_LORA = 32
GATE_LORA = 128
RWKV_GN_EPS = 64e-5
MAMBA_INNER = 2048
MAMBA_HEAD = 64
MAMBA_HEADS = 32
MAMBA_GROUPS = 4
MAMBA_STATE = 128
MAMBA_CONV = 4
SSD_CHUNK = 128
N_EXPERT_GROUPS = 4
EXPERTS_PER_GROUP = 8
N_EXPERTS = 32
EXPERT_FF = 512
NORM_EPS = 1e-6
MAMBA_NORM_EPS = 1e-5

LANES = 128
SUBLANES = 8
D_TILES = D_MODEL // LANES
N_PAIRS = D_MODEL // LANES
WKV_CHUNK = 64
WKV_BATCH = 4
EXPERT_ROWS = 512

COL_Z = 0
COL_GATE = 2048
COL_XS = 4096
COL_R = 6144
COL_K = 7168
COL_V = 8192
COL_BC = 9216
COL_WA = 10240
COL_GLO = 10368
COL_DTV = 10496
N_COLS = 10752
VLO_LANE = 32
INPROJ_TN = N_COLS // 7

VMEM_LIMIT = 56 * 1024 * 1024


def _softplus(x):
    return jnp.maximum(x, 0.0) + jnp.log(1.0 + jnp.exp(-jnp.abs(x)))


def _sigmoid(x):
    return 1.0 / (1.0 + jnp.exp(-x))


def _dot(a, b):
    return jnp.dot(a, b, preferred_element_type=F32)


def _dot_nt(a, b):
    return lax.dot_general(a, b, (((1,), (1,)), ((), ())), preferred_element_type=F32)


def _split_hi_lo(x):
    hi = x.astype(BF16)
    lo = (x - hi.astype(F32)).astype(BF16)
    return hi, lo


def _inproj_kernel(x_ref, g_ref, w_ref, o_ref, h_scr):
    @pl.when(pl.program_id(1) == 0)
    def _():
        x = x_ref[...]
        ms = jnp.mean(x * x, axis=-1, keepdims=True)
        h_scr[...] = (x * lax.rsqrt(ms + NORM_EPS) * g_ref[...]).astype(BF16)

    o_ref[...] = _dot(h_scr[...], w_ref[...]).astype(o_ref.dtype)


def _inproj(x2d, g_row, w_bf16, tm, tn):
    t, d = x2d.shape
    nc = w_bf16.shape[1]
    return pl.pallas_call(
        _inproj_kernel,
        out_shape=jax.ShapeDtypeStruct((t, nc), BF16),
        grid=(t // tm, nc // tn),
        in_specs=[
            pl.BlockSpec((tm, d), lambda i, j: (i, 0)),
            pl.BlockSpec((1, d), lambda i, j: (0, 0)),
            pl.BlockSpec((d, tn), lambda i, j: (0, j)),
        ],
        out_specs=pl.BlockSpec((tm, tn), lambda i, j: (i, j)),
        scratch_shapes=[pltpu.VMEM((tm, d), BF16)],
        compiler_params=pltpu.CompilerParams(
            dimension_semantics=("arbitrary", "arbitrary"), vmem_limit_bytes=VMEM_LIMIT),
        name="inproj",
    )(x2d, g_row, w_bf16)


RP_MU_R, RP_MU_K, RP_MU_V, RP_W0, RP_A0, RP_V0, RP_KK, RP_KA, RP_RK, RP_LNG, RP_LNB = range(11)
RP_ROWS = 16
SP_MU_WA, SP_MU_G, SP_MU_VL = range(3)
LR_DECAY, LR_ICLR, LR_GATE, LR_VAL = range(4)


def _bd_stack(x, m0):
    zero = jnp.zeros_like(x)
    return jnp.concatenate([jnp.where(m0, x, zero), jnp.where(m0, zero, x)], axis=0)


def _rwkv_kernel(has_vres, *refs):
    if has_vres:
        (r_ref, k_ref, v_ref, wa_ref, gl_ref, dtv_ref, vf_ref, rowp_ref, smallp_ref, lora_ref,
         y_ref, c_rkv, c_small, state, s_r, s_k, s_v, s_lw, s_wc, s_a, s_g) = refs
        vout_ref = None
    else:
        (r_ref, k_ref, v_ref, wa_ref, gl_ref, rowp_ref, smallp_ref, lora_ref,
         y_ref, vout_ref, c_rkv, c_small, state, s_r, s_k, s_v, s_lw, s_wc, s_a, s_g) = refs
        dtv_ref = vf_ref = None
    L = WKV_CHUNK
    c = pl.program_id(1)
    first = c == 0
    row = lax.broadcasted_iota(jnp.int32, (L, LANES), 0)
    lane = lax.broadcasted_iota(jnp.int32, (L, LANES), 1)

    @pl.when(first)
    def _():
        state[...] = jnp.zeros_like(state)
        c_rkv[...] = jnp.zeros_like(c_rkv)
        c_small[...] = jnp.zeros_like(c_small)

    def shift_mix(src, bb, carry_ref, idx, mu):
        u = src[bb].astype(F32)
        row0 = lax.broadcasted_iota(jnp.int32, u.shape, 0) == 0
        prev = jnp.where(row0, carry_ref[bb, idx, 0:1], pltpu.roll(u, 1, 0))
        carry_ref[bb, idx, 0:1] = u[L - 1:L]
        return u + (prev - u) * mu

    rowf = lambda slot: rowp_ref[slot:slot + 1]
    tril = (lax.broadcasted_iota(jnp.int32, (L, L), 0)
            >= lax.broadcasted_iota(jnp.int32, (L, L), 1)).astype(BF16)
    for bb in range(WKV_BATCH):
        s_r[bb] = shift_mix(r_ref, bb, c_rkv, 0, rowf(RP_MU_R))
        s_k[bb] = shift_mix(k_ref, bb, c_rkv, 1, rowf(RP_MU_K))
        xv = shift_mix(v_ref, bb, c_rkv, 2, rowf(RP_MU_V))
        xwa = shift_mix(wa_ref, bb, c_small, 0, smallp_ref[SP_MU_WA:SP_MU_WA + 1])
        xg = shift_mix(gl_ref, bb, c_small, 1, smallp_ref[SP_MU_G:SP_MU_G + 1])
        if has_vres:
            xvl = shift_mix(dtv_ref, bb, c_small, 2, smallp_ref[SP_MU_VL:SP_MU_VL + 1])
            mix = _sigmoid(rowf(RP_V0) + _dot(xvl.astype(BF16), lora_ref[LR_VAL]))
            xv = xv + (vf_ref[bb].astype(F32) - xv) * mix
        else:
            vout_ref[bb] = xv.astype(vout_ref.dtype)
        s_v[bb] = xv
        w_in = rowf(RP_W0) + _dot(jnp.tanh(xwa).astype(BF16), lora_ref[LR_DECAY])
        lw_all = -jnp.exp(-_softplus(-w_in) - 0.5)
        s_lw[bb] = lw_all
        lw_hi, lw_lo = _split_hi_lo(lw_all)
        s_wc[bb] = _dot(tril, lw_hi) + _dot(tril, lw_lo)
        s_a[bb] = _sigmoid(rowf(RP_A0) + _dot(xwa.astype(BF16), lora_ref[LR_ICLR]))
        s_g[bb] = _dot(_sigmoid(xg).astype(BF16), lora_ref[LR_GATE])

    m0 = lane < RWKV_HEAD
    s_in = lane % RWKV_HEAD
    strict = s_in < row
    incl = s_in <= row
    lane2 = lax.broadcasted_iota(jnp.int32, (2 * L, LANES), 1)
    row2 = lax.broadcasted_iota(jnp.int32, (2 * L, LANES), 0)
    bd_mask = (lane2 // RWKV_HEAD) == (row2 // RWKV_HEAD)
    bd_ones = bd_mask.astype(BF16)
    lane4 = lax.broadcasted_iota(jnp.int32, (4 * L, 2 * LANES), 1)
    row4 = lax.broadcasted_iota(jnp.int32, (4 * L, 2 * LANES), 0)
    bd_ones2 = ((lane4 // RWKV_HEAD) == (row4 // RWKV_HEAD)).astype(BF16)
    HB = L // 2
    lane_h = lax.broadcasted_iota(jnp.int32, (HB, LANES), 1)
    row_h = lax.broadcasted_iota(jnp.int32, (HB, LANES), 0)
    second_half = (lane_h % RWKV_HEAD) >= HB
    m0_h = lane_h < RWKV_HEAD
    eye_h = ((lane_h % HB) == row_h).astype(F32)
    blk_h = lane_h // HB
    zeros_h = jnp.zeros((HB, LANES), F32)

    def bd4(xh):
        return jnp.concatenate([jnp.where(blk_h == b, xh, 0.0) for b in range(4)], axis=0).astype(BF16)

    def rows_for_half(xh, half):
        parts = [jnp.where(m0_h, xh, 0.0), zeros_h, jnp.where(m0_h, 0.0, xh), zeros_h]
        if half == 1:
            parts = [zeros_h, parts[0], zeros_h, parts[2]]
        return jnp.concatenate(parts, axis=0).astype(BF16)

    def pair_steps(bb, p):
        sl = slice(p * LANES, (p + 1) * LANES)
        rp = lambda slot: rowp_ref[slot:slot + 1, sl]
        r = s_r[bb, :, sl]
        xk = s_k[bb, :, sl]
        v = s_v[bb, :, sl]
        lw = s_lw[bb, :, sl]
        w_cum = s_wc[bb, :, sl]
        a = s_a[bb, :, sl]
        kkr = xk * rp(RP_KK)
        k2 = xk * (1.0 + (a - 1.0) * rp(RP_KA))
        sums = _dot(jnp.concatenate([kkr * kkr, r * k2 * rp(RP_RK)], axis=1).astype(BF16), bd_ones2)
        yield
        kkn = kkr / jnp.maximum(jnp.sqrt(sums[:, 0:LANES]), 1e-12)
        bonus_s = sums[:, LANES:2 * LANES]
        avec = -kkn
        bvec = kkn * a
        w_mid = w_cum[L // 2 - 1:L // 2]
        w_end = w_cum[L - 1:L]
        e_abs = jnp.exp(w_cum)
        e_prev = jnp.exp(w_cum - lw)
        e_from_mid = jnp.exp(w_mid - w_cum)
        e_to_end = jnp.exp(w_end - w_cum)
        e_mid_inv = jnp.exp(-w_mid)
        r_abs = r * e_abs
        a_abs = avec * e_prev
        r_mid = r_abs * e_mid_inv
        a_mid = a_abs * e_mid_inv
        b_mid = bvec * e_from_mid
        k_mid = k2 * e_from_mid
        b_end = bvec * e_to_end
        k_end = k2 * e_to_end

        st = state[bb * N_PAIRS + p]
        lhs_abs = jnp.concatenate([a_abs, r_abs], axis=0).astype(BF16)
        m1 = _dot_nt(lhs_abs, st.astype(BF16))
        lhs_mid = jnp.concatenate([a_mid, r_mid], axis=0).astype(BF16)
        rhs_mid = jnp.concatenate([_bd_stack(b_mid, m0), _bd_stack(k_mid, m0)], axis=0).astype(BF16)
        m2 = _dot_nt(lhs_mid, rhs_mid)
        yield
        a_ab = jnp.where(strict, m2[0:L, 0:LANES], 0.0)
        a_ak = jnp.where(strict, m2[0:L, LANES:2 * LANES], 0.0)
        a_rb = jnp.where(incl, m2[L:2 * L, 0:LANES], 0.0)
        a_rk = jnp.where(incl, m2[L:2 * L, LANES:2 * LANES], 0.0)

        v_bd = _bd_stack(v, m0).astype(BF16)
        rhs_u = m1[0:L] + _dot(a_ak.astype(BF16), v_bd)

        top = a_ab[0:HB]
        bot = a_ab[HB:L]
        diag = jnp.where(second_half, bot, top)
        a21 = jnp.where(second_half, 0.0, bot)
        x_inv = eye_h + diag
        pw = _dot(diag.astype(BF16), bd4(diag))
        yield
        for _ in range(3):
            res = _dot(jnp.concatenate([x_inv, pw], axis=0).astype(BF16), bd4(pw))
            yield
            x_inv = x_inv + res[0:HB]
            pw = res[HB:L]
        x_upd = _dot(x_inv.astype(BF16), bd4(pw))
        yield
        x_inv_bf = (x_inv + x_upd).astype(BF16)
        u1 = _dot(x_inv_bf, rows_for_half(rhs_u[0:HB], 0))
        yield
        rhs2 = rhs_u[HB:L] + _dot(a21.astype(BF16), rows_for_half(u1, 0))
        yield
        u2 = _dot(x_inv_bf, rows_for_half(rhs2, 1))
        yield
        u = jnp.concatenate([u1, u2], axis=0)
        u_bd = _bd_stack(u, m0).astype(BF16)
        y = m1[L:2 * L] + _dot(jnp.concatenate([a_rb, a_rk], axis=1).astype(BF16),
                               jnp.concatenate([u_bd, v_bd], axis=0))

        uv_t = jnp.concatenate([u, v], axis=0).T.astype(BF16)
        upd = _dot(uv_t, jnp.concatenate([b_end, k_end], axis=0).astype(BF16))
        yield
        state[bb * N_PAIRS + p] = st * jnp.exp(w_end) + jnp.where(bd_mask, upd, 0.0)

        inv_n = 1.0 / RWKV_HEAD
        mean = _dot(y.astype(BF16), bd_ones) * inv_n
        yield
        yc = y - mean
        var = _dot((yc * yc).astype(BF16), bd_ones) * inv_n
        yield
        yn = yc * lax.rsqrt(var + RWKV_GN_EPS) * rp(RP_LNG) + rp(RP_LNB)
        y_ref[bb, :, sl] = ((yn + bonus_s * v) * s_g[bb, :, sl]).astype(y_ref.dtype)

    chains = [pair_steps(bb, p) for bb in range(WKV_BATCH) for p in range(N_PAIRS)]
    while chains:
        alive = []
        for chain in chains:
            try:
                next(chain)
                alive.append(chain)
            except StopIteration:
                pass
        chains = alive


def _rwkv_branch(proj3, v_first, rowp, smallp, lora):
    b, s, _ = proj3.shape
    L = WKV_CHUNK
    has_vres = v_first is not None
    nb = WKV_BATCH
    wide = lambda col: pl.BlockSpec((nb, L, D_MODEL), lambda i, j, c=col // D_MODEL: (i, j, c))
    narrow = lambda col: pl.BlockSpec((nb, L, LANES), lambda i, j, c=col // LANES: (i, j, c))
    full = lambda shape: pl.BlockSpec(shape, lambda i, j: (0,) * len(shape))
    out_blk = pl.BlockSpec((nb, L, D_MODEL), lambda i, j: (i, j, 0))
    in_specs = [wide(COL_R), wide(COL_K), wide(COL_V), narrow(COL_WA), narrow(COL_GLO)]
    args = [proj3] * 5
    if has_vres:
        in_specs += [narrow(COL_DTV), out_blk]
        args += [proj3, v_first]
    in_specs += [full(rowp.shape), full(smallp.shape), full(lora.shape)]
    args += [rowp, smallp, lora]
    wide_scr = lambda: pltpu.VMEM((nb, L, D_MODEL), F32)
    scratch = [pltpu.VMEM((nb, 3, SUBLANES, D_MODEL), F32), pltpu.VMEM((nb, 3, SUBLANES, LANES), F32),
               pltpu.VMEM((nb * N_PAIRS, 2 * RWKV_HEAD, LANES), F32)] + [wide_scr() for _ in range(7)]
    y_shape = jax.ShapeDtypeStruct((b, s, D_MODEL), BF16)
    out_shape = y_shape if has_vres else (y_shape, y_shape)
    out_specs = out_blk if has_vres else (out_blk, out_blk)
    return pl.pallas_call(
        functools.partial(_rwkv_kernel, has_vres),
        out_shape=out_shape,
        grid=(b // nb, s // L),
        in_specs=in_specs,
        out_specs=out_specs,
        scratch_shapes=scratch,
        compiler_params=pltpu.CompilerParams(
            dimension_semantics=("arbitrary", "arbitrary"), vmem_limit_bytes=VMEM_LIMIT),
        name="rwkv7",
    )(*args)


HEADS_PER_GROUP = MAMBA_HEADS // MAMBA_GROUPS
GROUP_WIDTH = MAMBA_INNER // MAMBA_GROUPS
BC_WIDTH = 2 * MAMBA_GROUPS * MAMBA_STATE
DP_DT_BIAS, DP_A_HEAD = range(2)
LOG2_E = 1.4426950408889634


def _silu(x):
    half = 0.5 * x
    return half + half * jnp.tanh(half)


def _mamba_kernel(z_ref, xs_ref, bc_ref, dtv_ref, cw_xs_ref, cb_xs_ref, cw_bc_ref, cb_bc_ref, dtp_ref,
                  dskip_ref, mnorm_ref, expand_ref, y_ref, tail_xs, tail_bc, state):
    L = SSD_CHUNK
    c = pl.program_id(1)
    first = c == 0

    @pl.when(first)
    def _():
        state[...] = jnp.zeros_like(state)
        tail_xs[...] = jnp.zeros_like(tail_xs)
        tail_bc[...] = jnp.zeros_like(tail_bc)

    def conv_silu(src_ref, tail_ref, w_ref, b_ref, cols):
        u = src_ref[0, :, cols].astype(F32)
        row8 = lax.broadcasted_iota(jnp.int32, (SUBLANES, u.shape[1]), 0)
        tail = tail_ref[:, cols]
        acc = u * w_ref[MAMBA_CONV - 1:MAMBA_CONV, cols] + b_ref[:, cols]
        for d in range(1, MAMBA_CONV):
            rolled = pltpu.roll(u, d, 0)
            head = jnp.where(row8 < d, pltpu.roll(tail, d, 0), rolled[0:SUBLANES])
            shifted = jnp.concatenate([head, rolled[SUBLANES:]], axis=0)
            acc = acc + shifted * w_ref[MAMBA_CONV - 1 - d:MAMBA_CONV - d, cols]
        tail_ref[:, cols] = u[L - SUBLANES:L]
        return _silu(acc)

    dt = _softplus(dtv_ref[0].astype(F32) + dtp_ref[DP_DT_BIAS:DP_DT_BIAS + 1])
    adt = dt * dtp_ref[DP_A_HEAD:DP_A_HEAD + 1]
    tril_b = lax.broadcasted_iota(jnp.int32, (L, L), 0) >= lax.broadcasted_iota(jnp.int32, (L, L), 1)
    tril = tril_b.astype(BF16)
    adt_hi, adt_lo = _split_hi_lo(adt)
    a_cs = _dot(tril, adt_hi) + _dot(tril, adt_lo)
    a_cs2 = a_cs * LOG2_E
    a_cs2_t = a_cs2.T
    dt_bf = dt.astype(BF16)
    exp_acs_bf = jnp.exp(a_cs).astype(BF16)
    to_end_bf = jnp.exp(a_cs[L - 1:L] - a_cs).astype(BF16)
    lane = lax.broadcasted_iota(jnp.int32, (L, LANES), 1)
    m0 = lane < MAMBA_HEAD

    for g in range(MAMBA_GROUPS):
        gs = slice(g * GROUP_WIDTH, (g + 1) * GROUP_WIDTH)
        b_cols = slice(g * MAMBA_STATE, (g + 1) * MAMBA_STATE)
        c_cols = slice((MAMBA_GROUPS + g) * MAMBA_STATE, (MAMBA_GROUPS + g + 1) * MAMBA_STATE)
        xc = conv_silu(xs_ref, tail_xs, cw_xs_ref, cb_xs_ref, gs)
        bg = conv_silu(bc_ref, tail_bc, cw_bc_ref, cb_bc_ref, b_cols)
        cg_bf = conv_silu(bc_ref, tail_bc, cw_bc_ref, cb_bc_ref, c_cols).astype(BF16)
        expand = expand_ref[:, gs]
        exp_acs_x = _dot(exp_acs_bf, expand)
        xdt = xc * _dot(dt_bf, expand)
        xdt_bf = xdt.astype(BF16)
        xw_bf = (xdt * _dot(to_end_bf, expand)).astype(BF16)
        cb = jnp.where(tril_b, _dot_nt(cg_bf, bg.astype(BF16)), 0.0)
        st = state[g]
        y_off = _dot(cg_bf, st.astype(BF16)) * exp_acs_x
        y_parts = []
        for jp in range(HEADS_PER_GROUP // 2):
            ws = []
            for hh in range(2):
                j = g * HEADS_PER_GROUP + 2 * jp + hh
                seg2 = a_cs2[:, j:j + 1] - a_cs2_t[j:j + 1, :]
                ws.append((cb * jnp.exp2(jnp.minimum(seg2, 0.0))).astype(BF16))
            ps = slice(jp * LANES, (jp + 1) * LANES)
            y_parts.append(_dot(jnp.concatenate(ws, axis=1), _bd_stack(xdt_bf[:, ps], m0)))
        state[g] = st * exp_acs_x[L - 1:L] + _dot(bg.T.astype(BF16), xw_bf)
        y = jnp.concatenate(y_parts, axis=1) + y_off + xc * dskip_ref[:, gs]
        y = y * _silu(z_ref[0, :, gs].astype(F32))
        ms = jnp.mean(y * y, axis=-1, keepdims=True)
        y_ref[0, :, gs] = (y * lax.rsqrt(ms + MAMBA_NORM_EPS) * mnorm_ref[:, gs]).astype(y_ref.dtype)


def _mamba_branch(proj3, cw_xs, cb_xs, cw_bc, cb_bc, dtp, dskip_x, mnorm, expand):
    b, s, _ = proj3.shape
    L = SSD_CHUNK
    blk = lambda width, col: pl.BlockSpec((1, L, width), lambda i, j, c=col // width: (i, j, c))
    full = lambda arr: pl.BlockSpec(arr.shape, lambda i, j: (0,) * arr.ndim)
    params = [cw_xs, cb_xs, cw_bc, cb_bc, dtp, dskip_x, mnorm, expand]
    return pl.pallas_call(
        _mamba_kernel,
        out_shape=jax.ShapeDtypeStruct((b, s, MAMBA_INNER), BF16),
        grid=(b, s // L),
        in_specs=[blk(MAMBA_INNER, COL_Z), blk(MAMBA_INNER, COL_XS), blk(BC_WIDTH, COL_BC),
                  blk(LANES, COL_DTV)] + [full(a) for a in params],
        out_specs=pl.BlockSpec((1, L, MAMBA_INNER), lambda i, j: (i, j, 0)),
        scratch_shapes=[pltpu.VMEM((SUBLANES, MAMBA_INNER), F32), pltpu.VMEM((SUBLANES, BC_WIDTH), F32),
                        pltpu.VMEM((MAMBA_GROUPS, MAMBA_STATE, GROUP_WIDTH), F32)],
        compiler_params=pltpu.CompilerParams(
            dimension_semantics=("arbitrary", "arbitrary"), vmem_limit_bytes=VMEM_LIMIT),
        name="mamba2_ssd",
    )(proj3, proj3, proj3, proj3, *params)


ROUTE_E0 = N_EXPERT_GROUPS
RI_E1, RI_E2, RI_RANK1, RI_RANK2 = range(4)
BIG = 1e30


def _merge_kernel(ya_ref, yb_ref, gate_ref, x_ref, wro_ref, wmo_ref, wout_ref, g2_ref, wr_ref, br_ref,
                  xo_ref, h2_ref, ri_ref, rw_ref, cnt_ref, run_cnt):
    tm = x_ref.shape[0]
    i = pl.program_id(0)

    @pl.when(i == 0)
    def _():
        run_cnt[...] = jnp.zeros_like(run_cnt)

    o_a = _dot(ya_ref[...], wro_ref[...])
    o_b = _dot(yb_ref[...], wmo_ref[...])
    gates = _sigmoid(gate_ref[...].astype(F32))
    merged = gates[:, 0:D_MODEL] * o_a + gates[:, D_MODEL:2 * D_MODEL] * o_b
    x = x_ref[...] + _dot(merged.astype(BF16), wout_ref[...])
    xo_ref[...] = x
    ms = jnp.mean(x * x, axis=-1, keepdims=True)
    h2 = x * lax.rsqrt(ms + NORM_EPS) * g2_ref[...]
    h2_ref[...] = h2

    logits = _dot(h2.astype(BF16), wr_ref[...]) + br_ref[...]
    lane = lax.broadcasted_iota(jnp.int32, (tm, LANES), 1)
    gmask = lane < N_EXPERT_GROUPS
    gl = jnp.where(gmask, logits, -BIG)
    gmax = jnp.max(gl, axis=-1, keepdims=True)
    gidx = jnp.min(jnp.where(gl == gmax, lane, LANES), axis=-1, keepdims=True)
    gsum = jnp.sum(jnp.where(gmask, jnp.exp(gl - gmax), 0.0), axis=-1, keepdims=True)
    g_p = 1.0 / gsum
    e_lo = ROUTE_E0 + gidx * EXPERTS_PER_GROUP
    emask = (lane >= e_lo) & (lane < e_lo + EXPERTS_PER_GROUP)
    el = jnp.where(emask, logits, -BIG)
    e1 = jnp.max(el, axis=-1, keepdims=True)
    i1 = jnp.min(jnp.where(el == e1, lane, LANES), axis=-1, keepdims=True)
    el2 = jnp.where(lane == i1, -BIG, el)
    e2 = jnp.max(el2, axis=-1, keepdims=True)
    i2 = jnp.min(jnp.where(el2 == e2, lane, LANES), axis=-1, keepdims=True)
    q = jnp.exp(e2 - e1)
    w1 = g_p / (1.0 + q)
    w2 = g_p * q / (1.0 + q)
    eid1 = i1 - ROUTE_E0
    eid2 = i2 - ROUTE_E0

    onehot = ((lane == eid1) | (lane == eid2)).astype(BF16)
    strict = (lax.broadcasted_iota(jnp.int32, (tm, tm), 0)
              > lax.broadcasted_iota(jnp.int32, (tm, tm), 1)).astype(BF16)
    before = _dot(strict, onehot) + run_cnt[0:1]
    rank1 = jnp.sum(jnp.where(lane == eid1, before, 0.0), axis=-1, keepdims=True).astype(jnp.int32)
    rank2 = jnp.sum(jnp.where(lane == eid2, before, 0.0), axis=-1, keepdims=True).astype(jnp.int32)
    total = run_cnt[0:1] + jnp.sum(onehot.astype(F32), axis=0, keepdims=True)
    run_cnt[0:1] = total
    cnt_ref[...] = jnp.broadcast_to(total, cnt_ref.shape)

    ri = jnp.where(lane == RI_E1, eid1, 0)
    ri = jnp.where(lane == RI_E2, eid2, ri)
    ri = jnp.where(lane == RI_RANK1, rank1, ri)
    ri = jnp.where(lane == RI_RANK2, rank2, ri)
    ri_ref[...] = ri
    rw_ref[...] = jnp.where(lane == 0, w1, jnp.where(lane == 1, w2, 0.0))


def _merge(ya, yb, proj, x2d, w_ro, w_mo, w_out, g2_row, w_r, b_r, tm):
    t = x2d.shape[0]
    full = lambda arr: pl.BlockSpec(arr.shape, lambda i: (0,) * arr.ndim, pipeline_mode=pl.Buffered(1))
    rows = lambda width: pl.BlockSpec((tm, width), lambda i: (i, 0))
    return pl.pallas_call(
        _merge_kernel,
        out_shape=(jax.ShapeDtypeStruct((t, D_MODEL), F32),
                   jax.ShapeDtypeStruct((t, D_MODEL), F32),
                   jax.ShapeDtypeStruct((t, LANES), jnp.int32),
                   jax.ShapeDtypeStruct((t, LANES), F32),
                   jax.ShapeDtypeStruct((SUBLANES, LANES), F32)),
        grid=(t // tm,),
        in_specs=[rows(D_MODEL), rows(MAMBA_INNER),
                  pl.BlockSpec((tm, 2 * D_MODEL), lambda i: (i, COL_GATE // (2 * D_MODEL))),
                  rows(D_MODEL), full(w_ro), full(w_mo), full(w_out), full(g2_row), full(w_r), full(b_r)],
        out_specs=(rows(D_MODEL), rows(D_MODEL),
                   rows(LANES), rows(LANES), pl.BlockSpec((SUBLANES, LANES), lambda i: (0, 0))),
        scratch_shapes=[pltpu.VMEM((SUBLANES, LANES), F32)],
        compiler_params=pltpu.CompilerParams(
            dimension_semantics=("arbitrary",), vmem_limit_bytes=VMEM_LIMIT),
        name="merge_route",
    )(ya, yb, proj, x2d, w_ro, w_mo, w_out, g2_row, w_r, b_r)


ROW_UNROLL = 8


def _fetch_step_indices(pos_hbm, idx_smem, idx_sem):
    i = pl.program_id(0)
    n = pl.num_programs(0)
    slot = i % 2
    per_step = pos_hbm.shape[1]

    def copy(step, s):
        dst = idx_smem.at[pl.ds(pl.multiple_of(s * per_step, per_step), per_step)]
        return pltpu.make_async_copy(pos_hbm.at[step], dst, idx_sem.at[s])

    @pl.when(i == 0)
    def _():
        copy(0, 0).start()

    copy(i, slot).wait()

    @pl.when(i + 1 < n)
    def _():
        copy(i + 1, 1 - slot).start()

    return slot * per_step


def _issue_row_copies(row_copy, n_groups):
    def issue(g, carry):
        for u in range(SUBLANES):
            for k in range(2):
                row_copy(g, u, k).start(priority=k)
        return carry

    lax.fori_loop(0, n_groups, issue, 0)


META_PEND, META_PCOUNT, META_N_USED = 0, N_EXPERTS, 2 * N_EXPERTS
META_SIZE = 2 * N_EXPERTS + 1


def _dispatch_kernel(min_used_blocks, meta_ref, pos_hbm, h2_ref, xb_out, idx_smem, zero_blk, idx_sem, row_sem,
                     zero_sem):
    n_groups = h2_ref.shape[0]
    tm = n_groups * SUBLANES
    n_blocks = xb_out.shape[0] // EXPERT_ROWS
    n_used = meta_ref[META_N_USED]

    @pl.when(pl.program_id(0) == 0)
    def _():
        zero_blk[...] = jnp.zeros_like(zero_blk)

        def zero_fill(start):
            dst = xb_out.at[pl.ds(pl.multiple_of(start, EXPERT_ROWS), EXPERT_ROWS)]
            return pltpu.make_async_copy(zero_blk, dst, zero_sem)

        fills = [(meta_ref[META_PCOUNT + e] > 0, meta_ref[META_PEND + e] - EXPERT_ROWS)
                 for e in range(N_EXPERTS)]
        fills += [(blk >= n_used, blk * EXPERT_ROWS) for blk in range(min_used_blocks, n_blocks)]
        for pred, start in fills:
            @pl.when(pred)
            def _():
                zero_fill(start).start()
        for pred, start in fills:
            @pl.when(pred)
            def _():
                zero_fill(start).wait()

    base = _fetch_step_indices(pos_hbm, idx_smem, idx_sem)

    def row_copy(g, u, k):
        row = idx_smem[base + 2 * SUBLANES * g + (2 * u + k)]
        return pltpu.make_async_copy(h2_ref.at[g, pl.ds(u, 1)], xb_out.at[pl.ds(row, 1)], row_sem)

    _issue_row_copies(row_copy, n_groups)
    for _ in range(2):
        pltpu.make_async_copy(xb_out.at[pl.ds(0, tm)], xb_out.at[pl.ds(0, tm)], row_sem).wait()


def _dispatch(meta, pos2, h2, n_rows, tm):
    t = h2.shape[0]
    min_used_blocks = (2 * t) // EXPERT_ROWS
    grid_spec = pltpu.PrefetchScalarGridSpec(
        num_scalar_prefetch=1,
        grid=(t // tm,),
        in_specs=[pl.BlockSpec(memory_space=pl.ANY),
                  pl.BlockSpec((tm // SUBLANES, SUBLANES, D_MODEL), lambda i, meta: (i, 0, 0))],
        out_specs=pl.BlockSpec(memory_space=pl.ANY),
        scratch_shapes=[pltpu.SMEM((4 * tm,), jnp.int32), pltpu.VMEM((EXPERT_ROWS, D_MODEL), F32),
                        pltpu.SemaphoreType.DMA((2,)), pltpu.SemaphoreType.DMA, pltpu.SemaphoreType.DMA],
    )
    return pl.pallas_call(
        functools.partial(_dispatch_kernel, min_used_blocks),
        out_shape=jax.ShapeDtypeStruct((n_rows, D_MODEL), F32),
        grid_spec=grid_spec,
        compiler_params=pltpu.CompilerParams(
            dimension_semantics=("arbitrary",), vmem_limit_bytes=VMEM_LIMIT),
        name="moe_dispatch",
    )(meta, pos2, h2.reshape(t // SUBLANES, SUBLANES, D_MODEL))


def _ffn_kernel(be_ref, xb_ref, w1_ref, w3_ref, w2_ref, yb_ref):
    n_used = be_ref[pl.num_programs(0)]

    @pl.when(pl.program_id(0) < n_used)
    def _():
        x = xb_ref[...].astype(BF16)
        h1 = _dot(x, w1_ref[0])
        h3 = _dot(x, w3_ref[0])
        hid = (_silu(h1) * h3).astype(BF16)
        yb_ref[...] = _dot(hid, w2_ref[0])

    @pl.when(pl.program_id(0) >= n_used)
    def _():
        yb_ref[...] = jnp.zeros_like(yb_ref)


def _expert_ffn(block_e, xb, w1, w3, w2):
    n_rows = xb.shape[0]
    br = EXPERT_ROWS
    grid_spec = pltpu.PrefetchScalarGridSpec(
        num_scalar_prefetch=1,
        grid=(n_rows // br,),
        in_specs=[pl.BlockSpec((br, D_MODEL), lambda i, be: (i, 0)),
                  pl.BlockSpec((1, D_MODEL, EXPERT_FF), lambda i, be: (be[i], 0, 0)),
                  pl.BlockSpec((1, D_MODEL, EXPERT_FF), lambda i, be: (be[i], 0, 0)),
                  pl.BlockSpec((1, EXPERT_FF, D_MODEL), lambda i, be: (be[i], 0, 0))],
        out_specs=pl.BlockSpec((br, D_MODEL), lambda i, be: (i, 0)),
    )
    return pl.pallas_call(
        _ffn_kernel,
        out_shape=jax.ShapeDtypeStruct(xb.shape, F32),
        grid_spec=grid_spec,
        compiler_params=pltpu.CompilerParams(
            dimension_semantics=("arbitrary",), vmem_limit_bytes=VMEM_LIMIT),
        name="expert_ffn",
    )(block_e, xb, w1, w3, w2)


def _combine_kernel(final_norm, pos_hbm, yb_hbm, x_ref, rw_ref, fg_ref, o_ref, idx_smem, buf, idx_sem, row_sem):
    tm = x_ref.shape[0]
    n_groups = tm // SUBLANES
    base = _fetch_step_indices(pos_hbm, idx_smem, idx_sem)

    def row_copy(g, u, k):
        row = idx_smem[base + 2 * SUBLANES * g + (2 * u + k)]
        return pltpu.make_async_copy(yb_hbm.at[pl.ds(row, 1)], buf.at[k, g, pl.ds(u, 1)], row_sem)

    _issue_row_copies(row_copy, n_groups)
    for _ in range(2):
        pltpu.make_async_copy(yb_hbm.at[pl.ds(0, tm)], yb_hbm.at[pl.ds(0, tm)], row_sem).wait()

    rw = rw_ref[...]
    y1 = buf[0].reshape(tm, D_MODEL)
    y2 = buf[1].reshape(tm, D_MODEL)
    x = x_ref[...] + rw[:, 0:1] * y1 + rw[:, 1:2] * y2
    if final_norm:
        ms = jnp.mean(x * x, axis=-1, keepdims=True)
        x = x * lax.rsqrt(ms + NORM_EPS) * fg_ref[...]
    o_ref[...] = x


def _combine(pos2, yb, x2d, rw, final_g_row, final_norm, tm):
    t = x2d.shape[0]
    return pl.pallas_call(
        functools.partial(_combine_kernel, final_norm),
        out_shape=jax.ShapeDtypeStruct(x2d.shape, F32),
        grid=(t // tm,),
        in_specs=[pl.BlockSpec(memory_space=pl.ANY), pl.BlockSpec(memory_space=pl.ANY),
                  pl.BlockSpec((tm, D_MODEL), lambda i: (i, 0)),
                  pl.BlockSpec((tm, LANES), lambda i: (i, 0)),
                  pl.BlockSpec((1, D_MODEL), lambda i: (0, 0))],
        out_specs=pl.BlockSpec((tm, D_MODEL), lambda i: (i, 0)),
        scratch_shapes=[pltpu.SMEM((4 * tm,), jnp.int32),
                        pltpu.VMEM((2, tm // SUBLANES, SUBLANES, D_MODEL), F32),
                        pltpu.SemaphoreType.DMA((2,)), pltpu.SemaphoreType.DMA],
        compiler_params=pltpu.CompilerParams(
            dimension_semantics=("arbitrary",), vmem_limit_bytes=VMEM_LIMIT),
        name="moe_combine",
    )(pos2, yb, x2d, rw, final_g_row)


def _pad_cols(w, width):
    return jnp.pad(w, ((0, 0), (0, width - w.shape[1])))


def _pack_w_in(w_in_l, w_v1_l):
    rw = 3 * D_MODEL
    o_wlo = rw
    o_alo = o_wlo + DECAY_LORA
    o_glo = o_alo + ICLR_LORA
    o_m = o_glo + GATE_LORA
    o_xbc = o_m + MAMBA_INNER
    o_dt = o_xbc + MAMBA_INNER + BC_WIDTH
    o_gate = o_dt + MAMBA_HEADS
    d = w_in_l.shape[0]
    vlo = w_v1_l if w_v1_l is not None else jnp.zeros((d, VALUE_LORA), w_in_l.dtype)
    dtv = _pad_cols(jnp.concatenate([w_in_l[:, o_dt:o_dt + MAMBA_HEADS], vlo], axis=1), LANES)
    cols = [
        w_in_l[:, o_m:o_m + MAMBA_INNER],
        w_in_l[:, o_gate:o_gate + 2 * D_MODEL],
        w_in_l[:, o_xbc:o_xbc + MAMBA_INNER],
        w_in_l[:, 0:rw],
        w_in_l[:, o_xbc + MAMBA_INNER:o_xbc + MAMBA_INNER + BC_WIDTH],
        w_in_l[:, o_wlo:o_glo],
        w_in_l[:, o_glo:o_m],
        dtv,
        jnp.zeros((d, N_COLS - COL_DTV - LANES), w_in_l.dtype),
    ]
    return jnp.concatenate(cols, axis=1).astype(BF16)


def _pack_rwkv_params(l, mu_rwkv, w0, a0, v0, k_k, k_a, r_k, lnx_g, lnx_b, w_decay2, w_iclr2, w_gate2,
                      mu_v, w_v2):
    mu = mu_rwkv[l]
    has_vres = l > 0
    zeros = jnp.zeros((D_MODEL,), F32)
    rows = [mu[0:D_MODEL], mu[D_MODEL:2 * D_MODEL], mu[2 * D_MODEL:3 * D_MODEL], w0[l], a0[l],
            v0[l - 1] if has_vres else zeros, k_k[l], k_a[l], r_k[l].reshape(-1), lnx_g[l], lnx_b[l]]
    rows += [zeros] * (RP_ROWS - len(rows))
    rowp = jnp.stack(rows, axis=0)
    rw = 3 * D_MODEL
    mu_vl = jnp.zeros((LANES,), F32)
    if has_vres:
        mu_vl = mu_vl.at[VLO_LANE:VLO_LANE + VALUE_LORA].set(mu_v[l - 1])
    smallp = jnp.stack([mu[rw:rw + LANES], mu[rw + LANES:rw + 2 * LANES], mu_vl]
                       + [jnp.zeros((LANES,), F32)] * (SUBLANES - 3), axis=0)
    z = lambda n: jnp.zeros((n, D_MODEL), F32)
    wd = jnp.concatenate([w_decay2[l], z(LANES - DECAY_LORA)], axis=0)
    wi = jnp.concatenate([z(DECAY_LORA), w_iclr2[l]], axis=0)
    wg = w_gate2[l]
    if has_vres:
        wv = jnp.concatenate([z(VLO_LANE), w_v2[l - 1], z(LANES - VLO_LANE - VALUE_LORA)], axis=0)
    else:
        wv = z(LANES)
    lora = jnp.stack([wd, wi, wg, wv], axis=0)
    return rowp, smallp, lora.astype(BF16)


def _pack_mamba_params(l, conv_w, conv_b, dt_bias, a_log, d_skip, mnorm_g):
    cw = conv_w[l].astype(F32)
    cb = conv_b[l].reshape(1, -1)
    cw_xs = jnp.concatenate([cw[:, :MAMBA_INNER], jnp.zeros((SUBLANES - MAMBA_CONV, MAMBA_INNER), F32)], axis=0)
    cw_bc = jnp.concatenate([cw[:, MAMBA_INNER:], jnp.zeros((SUBLANES - MAMBA_CONV, BC_WIDTH), F32)], axis=0)
    pad = lambda v: jnp.pad(v, (0, LANES - MAMBA_HEADS))
    dtp = jnp.stack([pad(dt_bias[l]), pad(-jnp.exp(a_log[l].astype(F32)))]
                    + [jnp.zeros((LANES,), F32)] * (SUBLANES - 2), axis=0)
    dskip_x = jnp.repeat(d_skip[l], MAMBA_HEAD).reshape(1, MAMBA_INNER)
    expand = (jnp.arange(LANES)[:, None] == (jnp.arange(MAMBA_INNER)[None, :] // MAMBA_HEAD)).astype(BF16)
    return cw_xs, cb[:, :MAMBA_INNER], cw_bc, cb[:, MAMBA_INNER:], dtp, dskip_x, mnorm_g[l].reshape(1, -1), expand


def _route_positions(ri, cnt, n_tiles, tile, n_blocks):
    counts = cnt[0, :N_EXPERTS].astype(jnp.int32)
    pcounts = (counts + EXPERT_ROWS - 1) // EXPERT_ROWS * EXPERT_ROWS
    pends = jnp.cumsum(pcounts)
    poffsets = pends - pcounts
    eid = ri[:, RI_E1:RI_E2 + 1]
    onehot = eid[:, :, None] == jnp.arange(N_EXPERTS, dtype=jnp.int32)
    pos = jnp.sum(jnp.where(onehot, poffsets, 0), axis=-1) + ri[:, RI_RANK1:RI_RANK2 + 1]
    n_used = pends[N_EXPERTS - 1:] // EXPERT_ROWS
    meta = jnp.concatenate([pends, pcounts, n_used]).astype(jnp.int32)
    block_start = jnp.arange(n_blocks, dtype=jnp.int32) * EXPERT_ROWS
    block_e = jnp.minimum(jnp.sum(block_start[:, None] >= pends[None, :], axis=1), N_EXPERTS - 1)
    block_e = jnp.concatenate([block_e.astype(jnp.int32), n_used.astype(jnp.int32)])
    return pos.reshape(n_tiles, 2 * tile), meta, block_e


def kernel(x, norm1_g, w_in, mu_rwkv, w0, w_decay2, a0, w_iclr2, w_gate2, k_k, k_a, r_k, lnx_g, lnx_b, w_rwkv_o, w_v1, mu_v, v0, w_v2, conv_w, conv_b, dt_bias, a_log, d_skip, mnorm_g, w_mamba_o, w_out, norm2_g, w_rg, b_rg, w_re, b_re, w_e1, w_e3, w_e2, final_g):
    bsz, seq, d = x.shape
    t = bsz * seq
    depth = w_in.shape[0]
    n_assign = 2 * t
    n_blocks = -(-(n_assign + N_EXPERTS * (EXPERT_ROWS - 1)) // EXPERT_ROWS)
    n_rows = n_blocks * EXPERT_ROWS
    tm_proj = min(1024, t)
    tm_merge = min(512, t)
    tm_moe = min(512, t)

    x2d = x.reshape(t, d)
    v_first = None
    for l in range(depth):
        w_cat = _pack_w_in(w_in[l], w_v1[l - 1] if l > 0 else None)
        proj = _inproj(x2d, norm1_g[l].reshape(1, d), w_cat, tm_proj, INPROJ_TN)
        proj3 = proj.reshape(bsz, seq, N_COLS)
        rowp, smallp, lora = _pack_rwkv_params(l, mu_rwkv, w0, a0, v0, k_k, k_a, r_k, lnx_g, lnx_b,
                                               w_decay2, w_iclr2, w_gate2, mu_v, w_v2)
        if l == 0:
            ya, v_first = _rwkv_branch(proj3, None, rowp, smallp, lora)
        else:
            ya = _rwkv_branch(proj3, v_first, rowp, smallp, lora)
        yb = _mamba_branch(proj3, *_pack_mamba_params(l, conv_w, conv_b, dt_bias, a_log, d_skip, mnorm_g))

        w_r = _pad_cols(jnp.concatenate([w_rg[l], w_re[l]], axis=1), LANES).astype(BF16)
        b_r = _pad_cols(jnp.concatenate([b_rg[l], b_re[l]]).reshape(1, -1), LANES)
        x2d, h2, ri, rw, cnt = _merge(
            ya.reshape(t, D_MODEL), yb.reshape(t, MAMBA_INNER), proj, x2d,
            w_rwkv_o[l].astype(BF16), w_mamba_o[l].astype(BF16), w_out[l].astype(BF16),
            norm2_g[l].reshape(1, d), w_r, b_r, tm_merge)

        pos2, meta, block_e = _route_positions(ri, cnt, t // tm_moe, tm_moe, n_blocks)
        xb = _dispatch(meta, pos2, h2, n_rows, tm_moe)
        ybuf = _expert_ffn(block_e, xb, w_e1[l].astype(BF16), w_e3[l].astype(BF16), w_e2[l].astype(BF16))
        x2d = _combine(pos2, ybuf, x2d, rw, final_g.reshape(1, d), l == depth - 1, tm_moe)
    return x2d.reshape(bsz, seq, d)
```

```python
import functools

import jax
import jax.numpy as jnp
from jax import lax
from jax.experimental import pallas as pl
from jax.experimental.pallas import tpu as pltpu

F32 = jnp.float32
BF16 = jnp.bfloat16

D_MODEL = 1024
RWKV_HEAD = 64
DECAY_LORA = 64
ICLR_LORA = 64
VALUE_LORA = 32
GATE_LORA = 128
RWKV_GN_EPS = 64e-5
MAMBA_INNER = 2048
MAMBA_HEAD = 64
MAMBA_HEADS = 32
MAMBA_GROUPS = 4
MAMBA_STATE = 128
MAMBA_CONV = 4
SSD_CHUNK = 128
N_EXPERT_GROUPS = 4
EXPERTS_PER_GROUP = 8
N_EXPERTS = 32
EXPERT_FF = 512
NORM_EPS = 1e-6
MAMBA_NORM_EPS = 1e-5

LANES = 128
SUBLANES = 8
D_TILES = D_MODEL // LANES
N_PAIRS = D_MODEL // LANES
WKV_CHUNK = 64
WKV_BATCH = 4
EXPERT_ROWS = 512

COL_Z = 0
COL_GATE = 2048
COL_XS = 4096
COL_R = 6144
COL_K = 7168
COL_V = 8192
COL_BC = 9216
COL_WA = 10240
COL_GLO = 10368
COL_DTV = 10496
N_COLS = 10752
VLO_LANE = 32

INPROJ_TM = 1024
INPROJ_TN = N_COLS // 7
MERGE_TM = 512
MOE_TM = 1024

V7X_VMEM_BYTES = 64 * 1024 * 1024
VMEM_LIMIT = V7X_VMEM_BYTES * 7 // 8


def _softplus(x):
    return jnp.maximum(x, 0.0) + jnp.log(1.0 + jnp.exp(-jnp.abs(x)))


def _sigmoid(x):
    return 1.0 / (1.0 + jnp.exp(-x))


def _dot(a, b):
    return jnp.dot(a, b, preferred_element_type=F32)


def _dot_nt(a, b):
    return lax.dot_general(a, b, (((1,), (1,)), ((), ())), preferred_element_type=F32)


def _split_hi_lo(x):
    hi = x.astype(BF16)
    lo = (x - hi.astype(F32)).astype(BF16)
    return hi, lo


def _inproj_kernel(x_ref, g_ref, w_ref, o_ref, h_scr):
    @pl.when(pl.program_id(1) == 0)
    def _():
        x = x_ref[...]
        ms = jnp.mean(x * x, axis=-1, keepdims=True)
        h_scr[...] = (x * lax.rsqrt(ms + NORM_EPS) * g_ref[...]).astype(BF16)

    o_ref[...] = _dot(h_scr[...], w_ref[...]).astype(o_ref.dtype)


def _inproj(x2d, g_row, w_bf16, tm, tn):
    t, d = x2d.shape
    nc = w_bf16.shape[1]
    return pl.pallas_call(
        _inproj_kernel,
        out_shape=jax.ShapeDtypeStruct((t, nc), BF16),
        grid=(t // tm, nc // tn),
        in_specs=[
            pl.BlockSpec((tm, d), lambda i, j: (i, 0)),
            pl.BlockSpec((1, d), lambda i, j: (0, 0)),
            pl.BlockSpec((d, tn), lambda i, j: (0, j)),
        ],
        out_specs=pl.BlockSpec((tm, tn), lambda i, j: (i, j)),
        scratch_shapes=[pltpu.VMEM((tm, d), BF16)],
        compiler_params=pltpu.CompilerParams(
            dimension_semantics=("arbitrary", "arbitrary"), vmem_limit_bytes=VMEM_LIMIT),
        name="inproj",
    )(x2d, g_row, w_bf16)


RP_MU_R, RP_MU_K, RP_MU_V, RP_W0, RP_A0, RP_V0, RP_KK, RP_KA, RP_RK, RP_LNG, RP_LNB = range(11)
RP_ROWS = 16
SP_MU_WA, SP_MU_G, SP_MU_VL = range(3)
LR_DECAY, LR_ICLR, LR_GATE, LR_VAL = range(4)


def _bd_stack(x, m0):
    zero = jnp.zeros_like(x)
    return jnp.concatenate([jnp.where(m0, x, zero), jnp.where(m0, zero, x)], axis=0)


def _rwkv_kernel(has_vres, nb, *refs):
    if has_vres:
        (r_ref, k_ref, v_ref, wa_ref, gl_ref, dtv_ref, vf_ref, rowp_ref, smallp_ref, lora_ref,
         y_ref, c_rkv, c_small, state, s_r, s_k, s_v, s_lw, s_wc, s_a, s_g) = refs
        vout_ref = None
    else:
        (r_ref, k_ref, v_ref, wa_ref, gl_ref, rowp_ref, smallp_ref, lora_ref,
         y_ref, vout_ref, c_rkv, c_small, state, s_r, s_k, s_v, s_lw, s_wc, s_a, s_g) = refs
        dtv_ref = vf_ref = None
    L = WKV_CHUNK
    c = pl.program_id(1)
    first = c == 0
    row = lax.broadcasted_iota(jnp.int32, (L, LANES), 0)
    lane = lax.broadcasted_iota(jnp.int32, (L, LANES), 1)

    @pl.when(first)
    def _():
        state[...] = jnp.zeros_like(state)
        c_rkv[...] = jnp.zeros_like(c_rkv)
        c_small[...] = jnp.zeros_like(c_small)

    def shift_mix(src, bb, carry_ref, idx, mu):
        u = src[bb].astype(F32)
        row0 = lax.broadcasted_iota(jnp.int32, u.shape, 0) == 0
        prev = jnp.where(row0, carry_ref[bb, idx, 0:1], pltpu.roll(u, 1, 0))
        carry_ref[bb, idx, 0:1] = u[L - 1:L]
        return u + (prev - u) * mu

    rowf = lambda slot: rowp_ref[slot:slot + 1]
    tril = (lax.broadcasted_iota(jnp.int32, (L, L), 0)
            >= lax.broadcasted_iota(jnp.int32, (L, L), 1)).astype(BF16)
    for bb in range(nb):
        s_r[bb] = shift_mix(r_ref, bb, c_rkv, 0, rowf(RP_MU_R))
        s_k[bb] = shift_mix(k_ref, bb, c_rkv, 1, rowf(RP_MU_K))
        xv = shift_mix(v_ref, bb, c_rkv, 2, rowf(RP_MU_V))
        xwa = shift_mix(wa_ref, bb, c_small, 0, smallp_ref[SP_MU_WA:SP_MU_WA + 1])
        xg = shift_mix(gl_ref, bb, c_small, 1, smallp_ref[SP_MU_G:SP_MU_G + 1])
        if has_vres:
            xvl = shift_mix(dtv_ref, bb, c_small, 2, smallp_ref[SP_MU_VL:SP_MU_VL + 1])
            mix = _sigmoid(rowf(RP_V0) + _dot(xvl.astype(BF16), lora_ref[LR_VAL]))
            xv = xv + (vf_ref[bb].astype(F32) - xv) * mix
        else:
            vout_ref[bb] = xv.astype(vout_ref.dtype)
        s_v[bb] = xv
        w_in = rowf(RP_W0) + _dot(jnp.tanh(xwa).astype(BF16), lora_ref[LR_DECAY])
        lw_all = -LOG2_E * jnp.exp(-_softplus(-w_in) - 0.5)
        s_lw[bb] = lw_all
        lw_hi, lw_lo = _split_hi_lo(lw_all)
        s_wc[bb] = _dot(tril, lw_hi) + _dot(tril, lw_lo)
        s_a[bb] = _sigmoid(rowf(RP_A0) + _dot(xwa.astype(BF16), lora_ref[LR_ICLR]))
        s_g[bb] = _dot(_sigmoid(xg).astype(BF16), lora_ref[LR_GATE])

    m0 = lane < RWKV_HEAD
    s_in = lane % RWKV_HEAD
    strict = s_in < row
    incl = s_in <= row
    lane2 = lax.broadcasted_iota(jnp.int32, (2 * L, LANES), 1)
    row2 = lax.broadcasted_iota(jnp.int32, (2 * L, LANES), 0)
    bd_mask = (lane2 // RWKV_HEAD) == (row2 // RWKV_HEAD)
    bd_ones = bd_mask.astype(BF16)
    lane4 = lax.broadcasted_iota(jnp.int32, (4 * L, 2 * LANES), 1)
    row4 = lax.broadcasted_iota(jnp.int32, (4 * L, 2 * LANES), 0)
    bd_ones2 = ((lane4 // RWKV_HEAD) == (row4 // RWKV_HEAD)).astype(BF16)
    HB = L // 2
    lane_h = lax.broadcasted_iota(jnp.int32, (HB, LANES), 1)
    row_h = lax.broadcasted_iota(jnp.int32, (HB, LANES), 0)
    second_half = (lane_h % RWKV_HEAD) >= HB
    m0_h = lane_h < RWKV_HEAD
    eye_h = ((lane_h % HB) == row_h).astype(F32)
    blk_h = lane_h // HB
    zeros_h = jnp.zeros((HB, LANES), F32)

    def bd4(xh):
        return jnp.concatenate([jnp.where(blk_h == b, xh, 0.0) for b in range(4)], axis=0).astype(BF16)

    def rows_for_half(xh, half):
        parts = [jnp.where(m0_h, xh, 0.0), zeros_h, jnp.where(m0_h, 0.0, xh), zeros_h]
        if half == 1:
            parts = [zeros_h, parts[0], zeros_h, parts[2]]
        return jnp.concatenate(parts, axis=0).astype(BF16)

    def pair_steps(bb, p):
        sl = slice(p * LANES, (p + 1) * LANES)
        rp = lambda slot: rowp_ref[slot:slot + 1, sl]
        r = s_r[bb, :, sl]
        xk = s_k[bb, :, sl]
        v = s_v[bb, :, sl]
        lw = s_lw[bb, :, sl]
        w_cum = s_wc[bb, :, sl]
        a = s_a[bb, :, sl]
        kkr = xk * rp(RP_KK)
        k2 = xk * (1.0 + (a - 1.0) * rp(RP_KA))
        sums = _dot(jnp.concatenate([kkr * kkr, r * k2 * rp(RP_RK)], axis=1).astype(BF16), bd_ones2)
        yield
        kkn = kkr / jnp.maximum(jnp.sqrt(sums[:, 0:LANES]), 1e-12)
        bonus_s = sums[:, LANES:2 * LANES]
        avec = -kkn
        bvec = kkn * a
        w_mid = w_cum[L // 2 - 1:L // 2]
        w_end = w_cum[L - 1:L]
        e_abs = jnp.exp2(w_cum)
        e_prev = jnp.exp2(w_cum - lw)
        e_from_mid = jnp.exp2(w_mid - w_cum)
        e_to_end = jnp.exp2(w_end - w_cum)
        e_mid_inv = jnp.exp2(-w_mid)
        r_abs = r * e_abs
        a_abs = avec * e_prev
        r_mid = r_abs * e_mid_inv
        a_mid = a_abs * e_mid_inv
        b_mid = bvec * e_from_mid
        k_mid = k2 * e_from_mid
        b_end = bvec * e_to_end
        k_end = k2 * e_to_end

        st = state[bb * N_PAIRS + p]
        lhs_abs = jnp.concatenate([a_abs, r_abs], axis=0).astype(BF16)
        m1 = _dot_nt(lhs_abs, st.astype(BF16))
        lhs_mid = jnp.concatenate([a_mid, r_mid], axis=0).astype(BF16)
        rhs_mid = jnp.concatenate([_bd_stack(b_mid, m0), _bd_stack(k_mid, m0)], axis=0).astype(BF16)
        m2 = _dot_nt(lhs_mid, rhs_mid)
        yield
        a_ab = jnp.where(strict, m2[0:L, 0:LANES], 0.0)
        a_ak = jnp.where(strict, m2[0:L, LANES:2 * LANES], 0.0)
        a_rb = jnp.where(incl, m2[L:2 * L, 0:LANES], 0.0)
        a_rk = jnp.where(incl, m2[L:2 * L, LANES:2 * LANES], 0.0)

        v_bd = _bd_stack(v, m0).astype(BF16)
        rhs_u = m1[0:L] + _dot(a_ak.astype(BF16), v_bd)

        top = a_ab[0:HB]
        bot = a_ab[HB:L]
        diag = jnp.where(second_half, bot, top)
        a21 = jnp.where(second_half, 0.0, bot)
        x_inv = eye_h + diag
        pw = _dot(diag.astype(BF16), bd4(diag))
        yield
        for _ in range(3):
            res = _dot(jnp.concatenate([x_inv, pw], axis=0).astype(BF16), bd4(pw))
            yield
            x_inv = x_inv + res[0:HB]
            pw = res[HB:L]
        x_upd = _dot(x_inv.astype(BF16), bd4(pw))
        yield
        x_inv_bf = (x_inv + x_upd).astype(BF16)
        u1 = _dot(x_inv_bf, rows_for_half(rhs_u[0:HB], 0))
        yield
        rhs2 = rhs_u[HB:L] + _dot(a21.astype(BF16), rows_for_half(u1, 0))
        yield
        u2 = _dot(x_inv_bf, rows_for_half(rhs2, 1))
        yield
        u = jnp.concatenate([u1, u2], axis=0)
        u_bd = _bd_stack(u, m0).astype(BF16)
        y = m1[L:2 * L] + _dot(jnp.concatenate([a_rb, a_rk], axis=1).astype(BF16),
                               jnp.concatenate([u_bd, v_bd], axis=0))

        uv_t = jnp.concatenate([u, v], axis=0).T.astype(BF16)
        upd = _dot(uv_t, jnp.concatenate([b_end, k_end], axis=0).astype(BF16))
        yield
        state[bb * N_PAIRS + p] = st * jnp.exp2(w_end) + jnp.where(bd_mask, upd, 0.0)

        inv_n = 1.0 / RWKV_HEAD
        mean = _dot(y.astype(BF16), bd_ones) * inv_n
        yield
        yc = y - mean
        var = _dot((yc * yc).astype(BF16), bd_ones) * inv_n
        yield
        yn = yc * lax.rsqrt(var + RWKV_GN_EPS) * rp(RP_LNG) + rp(RP_LNB)
        y_ref[bb, :, sl] = ((yn + bonus_s * v) * s_g[bb, :, sl]).astype(y_ref.dtype)

    chains = [pair_steps(bb, p) for bb in range(nb) for p in range(N_PAIRS)]
    while chains:
        alive = []
        for chain in chains:
            try:
                next(chain)
                alive.append(chain)
            except StopIteration:
                pass
        chains = alive


def _rwkv_branch(proj3, v_first, rowp, smallp, lora, nb):
    b, s, _ = proj3.shape
    L = WKV_CHUNK
    has_vres = v_first is not None
    assert b % nb == 0 and s % L == 0, (b, s)
    wide = lambda col: pl.BlockSpec((nb, L, D_MODEL), lambda i, j, c=col // D_MODEL: (i, j, c))
    narrow = lambda col: pl.BlockSpec((nb, L, LANES), lambda i, j, c=col // LANES: (i, j, c))
    full = lambda shape: pl.BlockSpec(shape, lambda i, j: (0,) * len(shape))
    out_blk = pl.BlockSpec((nb, L, D_MODEL), lambda i, j: (i, j, 0))
    in_specs = [wide(COL_R), wide(COL_K), wide(COL_V), narrow(COL_WA), narrow(COL_GLO)]
    args = [proj3] * 5
    if has_vres:
        in_specs += [narrow(COL_DTV), out_blk]
        args += [proj3, v_first]
    in_specs += [full(rowp.shape), full(smallp.shape), full(lora.shape)]
    args += [rowp, smallp, lora]
    wide_scr = lambda: pltpu.VMEM((nb, L, D_MODEL), F32)
    scratch = [pltpu.VMEM((nb, 3, SUBLANES, D_MODEL), F32), pltpu.VMEM((nb, 3, SUBLANES, LANES), F32),
               pltpu.VMEM((nb * N_PAIRS, 2 * RWKV_HEAD, LANES), F32)] + [wide_scr() for _ in range(7)]
    y_shape = jax.ShapeDtypeStruct((b, s, D_MODEL), BF16)
    out_shape = y_shape if has_vres else (y_shape, y_shape)
    out_specs = out_blk if has_vres else (out_blk, out_blk)
    return pl.pallas_call(
        functools.partial(_rwkv_kernel, has_vres, nb),
        out_shape=out_shape,
        grid=(b // nb, s // L),
        in_specs=in_specs,
        out_specs=out_specs,
        scratch_shapes=scratch,
        compiler_params=pltpu.CompilerParams(
            dimension_semantics=("arbitrary", "arbitrary"), vmem_limit_bytes=VMEM_LIMIT),
        name="rwkv7",
    )(*args)


HEADS_PER_GROUP = MAMBA_HEADS // MAMBA_GROUPS
GROUP_WIDTH = MAMBA_INNER // MAMBA_GROUPS
BC_WIDTH = 2 * MAMBA_GROUPS * MAMBA_STATE
DP_DT_BIAS, DP_A_HEAD = range(2)
LOG2_E = 1.4426950408889634


def _silu(x):
    half = 0.5 * x
    return half + half * jnp.tanh(half)


def _mamba_kernel(z_ref, xs_ref, bc_ref, dtv_ref, cw_xs_ref, cb_xs_ref, cw_bc_ref, cb_bc_ref, dtp_ref,
                  dskip_ref, mnorm_ref, expand_ref, y_ref, tail_xs, tail_bc, state):
    L = SSD_CHUNK
    c = pl.program_id(1)
    first = c == 0

    @pl.when(first)
    def _():
        state[...] = jnp.zeros_like(state)
        tail_xs[...] = jnp.zeros_like(tail_xs)
        tail_bc[...] = jnp.zeros_like(tail_bc)

    def conv_silu(src_ref, tail_ref, w_ref, b_ref, cols):
        u = src_ref[0, :, cols].astype(F32)
        row8 = lax.broadcasted_iota(jnp.int32, (SUBLANES, u.shape[1]), 0)
        tail = tail_ref[:, cols]
        acc = u * w_ref[MAMBA_CONV - 1:MAMBA_CONV, cols] + b_ref[:, cols]
        for d in range(1, MAMBA_CONV):
            rolled = pltpu.roll(u, d, 0)
            head = jnp.where(row8 < d, pltpu.roll(tail, d, 0), rolled[0:SUBLANES])
            shifted = jnp.concatenate([head, rolled[SUBLANES:]], axis=0)
            acc = acc + shifted * w_ref[MAMBA_CONV - 1 - d:MAMBA_CONV - d, cols]
        tail_ref[:, cols] = u[L - SUBLANES:L]
        return _silu(acc)

    dt = _softplus(dtv_ref[0].astype(F32) + dtp_ref[DP_DT_BIAS:DP_DT_BIAS + 1])
    adt = dt * dtp_ref[DP_A_HEAD:DP_A_HEAD + 1]
    tril_b = lax.broadcasted_iota(jnp.int32, (L, L), 0) >= lax.broadcasted_iota(jnp.int32, (L, L), 1)
    tril = tril_b.astype(BF16)
    adt_hi, adt_lo = _split_hi_lo(adt)
    a_cs = _dot(tril, adt_hi) + _dot(tril, adt_lo)
    a_cs2 = a_cs * LOG2_E
    a_cs2_t = a_cs2.T
    dt_bf = dt.astype(BF16)
    exp_acs_bf = jnp.exp(a_cs).astype(BF16)
    to_end_bf = jnp.exp(a_cs[L - 1:L] - a_cs).astype(BF16)
    lane = lax.broadcasted_iota(jnp.int32, (L, LANES), 1)
    m0 = lane < MAMBA_HEAD

    for g in range(MAMBA_GROUPS):
        gs = slice(g * GROUP_WIDTH, (g + 1) * GROUP_WIDTH)
        b_cols = slice(g * MAMBA_STATE, (g + 1) * MAMBA_STATE)
        c_cols = slice((MAMBA_GROUPS + g) * MAMBA_STATE, (MAMBA_GROUPS + g + 1) * MAMBA_STATE)
        xc = conv_silu(xs_ref, tail_xs, cw_xs_ref, cb_xs_ref, gs)
        bg = conv_silu(bc_ref, tail_bc, cw_bc_ref, cb_bc_ref, b_cols)
        cg_bf = conv_silu(bc_ref, tail_bc, cw_bc_ref, cb_bc_ref, c_cols).astype(BF16)
        expand = expand_ref[:, gs]
        exp_acs_x = _dot(exp_acs_bf, expand)
        xdt = xc * _dot(dt_bf, expand)
        xdt_bf = xdt.astype(BF16)
        xw_bf = (xdt * _dot(to_end_bf, expand)).astype(BF16)
        cb = jnp.where(tril_b, _dot_nt(cg_bf, bg.astype(BF16)), 0.0)
        st = state[g]
        y_off = _dot(cg_bf, st.astype(BF16)) * exp_acs_x
        y_parts = []
        for jp in range(HEADS_PER_GROUP // 2):
            ws = []
            for hh in range(2):
                j = g * HEADS_PER_GROUP + 2 * jp + hh
                seg2 = a_cs2[:, j:j + 1] - a_cs2_t[j:j + 1, :]
                ws.append((cb * jnp.exp2(jnp.minimum(seg2, 0.0))).astype(BF16))
            ps = slice(jp * LANES, (jp + 1) * LANES)
            y_parts.append(_dot(jnp.concatenate(ws, axis=1), _bd_stack(xdt_bf[:, ps], m0)))
        state[g] = st * exp_acs_x[L - 1:L] + _dot(bg.T.astype(BF16), xw_bf)
        y = jnp.concatenate(y_parts, axis=1) + y_off + xc * dskip_ref[:, gs]
        y = y * _silu(z_ref[0, :, gs].astype(F32))
        ms = jnp.mean(y * y, axis=-1, keepdims=True)
        y_ref[0, :, gs] = (y * lax.rsqrt(ms + MAMBA_NORM_EPS) * mnorm_ref[:, gs]).astype(y_ref.dtype)


def _mamba_branch(proj3, cw_xs, cb_xs, cw_bc, cb_bc, dtp, dskip_x, mnorm, expand):
    b, s, _ = proj3.shape
    L = SSD_CHUNK
    blk = lambda width, col: pl.BlockSpec((1, L, width), lambda i, j, c=col // width: (i, j, c))
    full = lambda arr: pl.BlockSpec(arr.shape, lambda i, j: (0,) * arr.ndim)
    params = [cw_xs, cb_xs, cw_bc, cb_bc, dtp, dskip_x, mnorm, expand]
    return pl.pallas_call(
        _mamba_kernel,
        out_shape=jax.ShapeDtypeStruct((b, s, MAMBA_INNER), BF16),
        grid=(b, s // L),
        in_specs=[blk(MAMBA_INNER, COL_Z), blk(MAMBA_INNER, COL_XS), blk(BC_WIDTH, COL_BC),
                  blk(LANES, COL_DTV)] + [full(a) for a in params],
        out_specs=pl.BlockSpec((1, L, MAMBA_INNER), lambda i, j: (i, j, 0)),
        scratch_shapes=[pltpu.VMEM((SUBLANES, MAMBA_INNER), F32), pltpu.VMEM((SUBLANES, BC_WIDTH), F32),
                        pltpu.VMEM((MAMBA_GROUPS, MAMBA_STATE, GROUP_WIDTH), F32)],
        compiler_params=pltpu.CompilerParams(
            dimension_semantics=("arbitrary", "arbitrary"), vmem_limit_bytes=VMEM_LIMIT),
        name="mamba2_ssd",
    )(proj3, proj3, proj3, proj3, *params)


ROUTE_E0 = N_EXPERT_GROUPS
RI_E1, RI_E2, RI_RANK1, RI_RANK2 = range(4)
BIG = 1e30


def _merge_kernel(ya_ref, yb_ref, gate_ref, x_ref, wro_ref, wmo_ref, wout_ref, g2_ref, wr_ref, br_ref,
                  xo_ref, h2_ref, ri_ref, rw_ref, cnt_ref, run_cnt):
    tm = x_ref.shape[0]
    i = pl.program_id(0)

    @pl.when(i == 0)
    def _():
        run_cnt[...] = jnp.zeros_like(run_cnt)

    o_a = _dot(ya_ref[...], wro_ref[...])
    o_b = _dot(yb_ref[...], wmo_ref[...])
    gates = _sigmoid(gate_ref[...].astype(F32))
    merged = gates[:, 0:D_MODEL] * o_a + gates[:, D_MODEL:2 * D_MODEL] * o_b
    x = x_ref[...] + _dot(merged.astype(BF16), wout_ref[...])
    xo_ref[...] = x
    ms = jnp.mean(x * x, axis=-1, keepdims=True)
    h2 = x * lax.rsqrt(ms + NORM_EPS) * g2_ref[...]
    h2_ref[...] = h2

    logits = _dot(h2.astype(BF16), wr_ref[...]) + br_ref[...]
    lane = lax.broadcasted_iota(jnp.int32, (tm, LANES), 1)
    gmask = lane < N_EXPERT_GROUPS
    gl = jnp.where(gmask, logits, -BIG)
    gmax = jnp.max(gl, axis=-1, keepdims=True)
    gidx = jnp.min(jnp.where(gl == gmax, lane, LANES), axis=-1, keepdims=True)
    gsum = jnp.sum(jnp.where(gmask, jnp.exp(gl - gmax), 0.0), axis=-1, keepdims=True)
    g_p = 1.0 / gsum
    e_lo = ROUTE_E0 + gidx * EXPERTS_PER_GROUP
    emask = (lane >= e_lo) & (lane < e_lo + EXPERTS_PER_GROUP)
    el = jnp.where(emask, logits, -BIG)
    e1 = jnp.max(el, axis=-1, keepdims=True)
    i1 = jnp.min(jnp.where(el == e1, lane, LANES), axis=-1, keepdims=True)
    el2 = jnp.where(lane == i1, -BIG, el)
    e2 = jnp.max(el2, axis=-1, keepdims=True)
    i2 = jnp.min(jnp.where(el2 == e2, lane, LANES), axis=-1, keepdims=True)
    q = jnp.exp(e2 - e1)
    w1 = g_p / (1.0 + q)
    w2 = g_p * q / (1.0 + q)
    eid1 = i1 - ROUTE_E0
    eid2 = i2 - ROUTE_E0

    onehot = ((lane == eid1) | (lane == eid2)).astype(BF16)
    strict = (lax.broadcasted_iota(jnp.int32, (tm, tm), 0)
              > lax.broadcasted_iota(jnp.int32, (tm, tm), 1)).astype(BF16)
    before = _dot(strict, onehot) + run_cnt[0:1]
    rank1 = jnp.sum(jnp.where(lane == eid1, before, 0.0), axis=-1, keepdims=True).astype(jnp.int32)
    rank2 = jnp.sum(jnp.where(lane == eid2, before, 0.0), axis=-1, keepdims=True).astype(jnp.int32)
    total = run_cnt[0:1] + jnp.sum(onehot.astype(F32), axis=0, keepdims=True)
    run_cnt[0:1] = total
    cnt_ref[...] = jnp.broadcast_to(total, cnt_ref.shape)

    ri = jnp.where(lane == RI_E1, eid1, 0)
    ri = jnp.where(lane == RI_E2, eid2, ri)
    ri = jnp.where(lane == RI_RANK1, rank1, ri)
    ri = jnp.where(lane == RI_RANK2, rank2, ri)
    ri_ref[...] = ri
    rw_ref[...] = jnp.where(lane == 0, w1, jnp.where(lane == 1, w2, 0.0))


def _merge(ya, yb, proj, x2d, w_ro, w_mo, w_out, g2_row, w_r, b_r, tm):
    t = x2d.shape[0]
    full = lambda arr: pl.BlockSpec(arr.shape, lambda i: (0,) * arr.ndim, pipeline_mode=pl.Buffered(1))
    rows = lambda width: pl.BlockSpec((tm, width), lambda i: (i, 0))
    return pl.pallas_call(
        _merge_kernel,
        out_shape=(jax.ShapeDtypeStruct((t, D_MODEL), F32),
                   jax.ShapeDtypeStruct((t, D_MODEL), F32),
                   jax.ShapeDtypeStruct((t, LANES), jnp.int32),
                   jax.ShapeDtypeStruct((t, LANES), F32),
                   jax.ShapeDtypeStruct((SUBLANES, LANES), F32)),
        grid=(t // tm,),
        in_specs=[rows(D_MODEL), rows(MAMBA_INNER),
                  pl.BlockSpec((tm, 2 * D_MODEL), lambda i: (i, COL_GATE // (2 * D_MODEL))),
                  rows(D_MODEL), full(w_ro), full(w_mo), full(w_out), full(g2_row), full(w_r), full(b_r)],
        out_specs=(rows(D_MODEL), rows(D_MODEL),
                   rows(LANES), rows(LANES), pl.BlockSpec((SUBLANES, LANES), lambda i: (0, 0))),
        scratch_shapes=[pltpu.VMEM((SUBLANES, LANES), F32)],
        compiler_params=pltpu.CompilerParams(
            dimension_semantics=("arbitrary",), vmem_limit_bytes=VMEM_LIMIT),
        name="merge_route",
    )(ya, yb, proj, x2d, w_ro, w_mo, w_out, g2_row, w_r, b_r)


def _fetch_step_indices(pos_hbm, idx_smem, idx_sem):
    i = pl.program_id(0)
    n = pl.num_programs(0)
    slot = i % 2
    per_step = pos_hbm.shape[1]

    def copy(step, s):
        dst = idx_smem.at[pl.ds(pl.multiple_of(s * per_step, per_step), per_step)]
        return pltpu.make_async_copy(pos_hbm.at[step], dst, idx_sem.at[s])

    @pl.when(i == 0)
    def _():
        copy(0, 0).start()

    copy(i, slot).wait()

    @pl.when(i + 1 < n)
    def _():
        copy(i + 1, 1 - slot).start()

    return slot * per_step


def _issue_row_copies(row_copy, n_groups):
    def issue(g, carry):
        for u in range(SUBLANES):
            for k in range(2):
                row_copy(g, u, k).start(priority=k)
        return carry

    lax.fori_loop(0, n_groups, issue, 0)


META_PEND, META_PCOUNT, META_N_USED = 0, N_EXPERTS, 2 * N_EXPERTS
META_SIZE = 2 * N_EXPERTS + 1


def _dispatch_kernel(min_used_blocks, meta_ref, pos_hbm, h2_ref, xb_out, idx_smem, zero_blk, idx_sem, row_sem,
                     zero_sem):
    n_groups = h2_ref.shape[0]
    tm = n_groups * SUBLANES
    n_blocks = xb_out.shape[0] // EXPERT_ROWS
    n_used = meta_ref[META_N_USED]

    @pl.when(pl.program_id(0) == 0)
    def _():
        zero_blk[...] = jnp.zeros_like(zero_blk)

        def zero_fill(start):
            dst = xb_out.at[pl.ds(pl.multiple_of(start, EXPERT_ROWS), EXPERT_ROWS)]
            return pltpu.make_async_copy(zero_blk, dst, zero_sem)

        fills = [(meta_ref[META_PCOUNT + e] > 0, meta_ref[META_PEND + e] - EXPERT_ROWS)
                 for e in range(N_EXPERTS)]
        fills += [(blk >= n_used, blk * EXPERT_ROWS) for blk in range(min_used_blocks, n_blocks)]
        for pred, start in fills:
            @pl.when(pred)
            def _():
                zero_fill(start).start()
        for pred, start in fills:
            @pl.when(pred)
            def _():
                zero_fill(start).wait()

    base = _fetch_step_indices(pos_hbm, idx_smem, idx_sem)

    def row_copy(g, u, k):
        row = idx_smem[base + 2 * SUBLANES * g + (2 * u + k)]
        return pltpu.make_async_copy(h2_ref.at[g, pl.ds(u, 1)], xb_out.at[pl.ds(row, 1)], row_sem)

    _issue_row_copies(row_copy, n_groups)
    for _ in range(2):
        pltpu.make_async_copy(xb_out.at[pl.ds(0, tm)], xb_out.at[pl.ds(0, tm)], row_sem).wait()


def _dispatch(meta, pos2, h2, n_rows, tm):
    t = h2.shape[0]
    min_used_blocks = (2 * t) // EXPERT_ROWS
    grid_spec = pltpu.PrefetchScalarGridSpec(
        num_scalar_prefetch=1,
        grid=(t // tm,),
        in_specs=[pl.BlockSpec(memory_space=pl.ANY),
                  pl.BlockSpec((tm // SUBLANES, SUBLANES, D_MODEL), lambda i, meta: (i, 0, 0))],
        out_specs=pl.BlockSpec(memory_space=pl.ANY),
        scratch_shapes=[pltpu.SMEM((4 * tm,), jnp.int32), pltpu.VMEM((EXPERT_ROWS, D_MODEL), F32),
                        pltpu.SemaphoreType.DMA((2,)), pltpu.SemaphoreType.DMA, pltpu.SemaphoreType.DMA],
    )
    return pl.pallas_call(
        functools.partial(_dispatch_kernel, min_used_blocks),
        out_shape=jax.ShapeDtypeStruct((n_rows, D_MODEL), F32),
        grid_spec=grid_spec,
        compiler_params=pltpu.CompilerParams(
            dimension_semantics=("arbitrary",), vmem_limit_bytes=VMEM_LIMIT),
        name="moe_dispatch",
    )(meta, pos2, h2.reshape(t // SUBLANES, SUBLANES, D_MODEL))


def _ffn_kernel(be_ref, xb_ref, w1_ref, w3_ref, w2_ref, yb_ref, w1_bf, w3_bf, w2_bf):
    i = pl.program_id(0)
    n_used = be_ref[pl.num_programs(0)]

    @pl.when((i == 0) | (be_ref[i] != be_ref[jnp.maximum(i - 1, 0)]))
    def _():
        w1_bf[...] = w1_ref[0, 0].astype(BF16)
        w3_bf[...] = w3_ref[0, 0].astype(BF16)
        w2_bf[...] = w2_ref[0, 0].astype(BF16)

    @pl.when(i < n_used)
    def _():
        x = xb_ref[...].astype(BF16)
        h1 = _dot(x, w1_bf[...])
        h3 = _dot(x, w3_bf[...])
        hid = (_silu(h1) * h3).astype(BF16)
        yb_ref[...] = _dot(hid, w2_bf[...])

    @pl.when(i >= n_used)
    def _():
        yb_ref[...] = jnp.zeros_like(yb_ref)


def _expert_ffn(block_e, xb, w1, w3, w2, layer):
    n_rows = xb.shape[0]
    br = EXPERT_ROWS
    grid_spec = pltpu.PrefetchScalarGridSpec(
        num_scalar_prefetch=1,
        grid=(n_rows // br,),
        in_specs=[pl.BlockSpec((br, D_MODEL), lambda i, be: (i, 0)),
                  pl.BlockSpec((1, 1, D_MODEL, EXPERT_FF), lambda i, be: (layer, be[i], 0, 0)),
                  pl.BlockSpec((1, 1, D_MODEL, EXPERT_FF), lambda i, be: (layer, be[i], 0, 0)),
                  pl.BlockSpec((1, 1, EXPERT_FF, D_MODEL), lambda i, be: (layer, be[i], 0, 0))],
        out_specs=pl.BlockSpec((br, D_MODEL), lambda i, be: (i, 0)),
        scratch_shapes=[pltpu.VMEM((D_MODEL, EXPERT_FF), BF16), pltpu.VMEM((D_MODEL, EXPERT_FF), BF16),
                        pltpu.VMEM((EXPERT_FF, D_MODEL), BF16)],
    )
    return pl.pallas_call(
        _ffn_kernel,
        out_shape=jax.ShapeDtypeStruct(xb.shape, F32),
        grid_spec=grid_spec,
        compiler_params=pltpu.CompilerParams(
            dimension_semantics=("arbitrary",), vmem_limit_bytes=VMEM_LIMIT),
        name="expert_ffn",
    )(block_e, xb, w1, w3, w2)


def _combine_kernel(final_norm, pos_hbm, yb_hbm, x_ref, rw_ref, fg_ref, o_ref, idx_smem, buf, idx_sem, row_sem):
    tm = x_ref.shape[0]
    n_groups = tm // SUBLANES
    base = _fetch_step_indices(pos_hbm, idx_smem, idx_sem)

    def row_copy(g, u, k):
        row = idx_smem[base + 2 * SUBLANES * g + (2 * u + k)]
        return pltpu.make_async_copy(yb_hbm.at[pl.ds(row, 1)], buf.at[k, g, pl.ds(u, 1)], row_sem)

    _issue_row_copies(row_copy, n_groups)
    for _ in range(2):
        pltpu.make_async_copy(yb_hbm.at[pl.ds(0, tm)], yb_hbm.at[pl.ds(0, tm)], row_sem).wait()

    rw = rw_ref[...]
    y1 = buf[0].reshape(tm, D_MODEL)
    y2 = buf[1].reshape(tm, D_MODEL)
    x = x_ref[...] + rw[:, 0:1] * y1 + rw[:, 1:2] * y2
    if final_norm:
        ms = jnp.mean(x * x, axis=-1, keepdims=True)
        x = x * lax.rsqrt(ms + NORM_EPS) * fg_ref[...]
    o_ref[...] = x


def _combine(pos2, yb, x2d, rw, final_g_row, final_norm, tm):
    t = x2d.shape[0]
    return pl.pallas_call(
        functools.partial(_combine_kernel, final_norm),
        out_shape=jax.ShapeDtypeStruct(x2d.shape, F32),
        grid=(t // tm,),
        in_specs=[pl.BlockSpec(memory_space=pl.ANY), pl.BlockSpec(memory_space=pl.ANY),
                  pl.BlockSpec((tm, D_MODEL), lambda i: (i, 0)),
                  pl.BlockSpec((tm, LANES), lambda i: (i, 0)),
                  pl.BlockSpec((1, D_MODEL), lambda i: (0, 0))],
        out_specs=pl.BlockSpec((tm, D_MODEL), lambda i: (i, 0)),
        scratch_shapes=[pltpu.SMEM((4 * tm,), jnp.int32),
                        pltpu.VMEM((2, tm // SUBLANES, SUBLANES, D_MODEL), F32),
                        pltpu.SemaphoreType.DMA((2,)), pltpu.SemaphoreType.DMA],
        compiler_params=pltpu.CompilerParams(
            dimension_semantics=("arbitrary",), vmem_limit_bytes=VMEM_LIMIT),
        name="moe_combine",
    )(pos2, yb, x2d, rw, final_g_row)


def _pad_cols(w, width):
    return jnp.pad(w, ((0, 0), (0, width - w.shape[1])))


def _pack_w_in(w_in_l, w_v1_l):
    rw = 3 * D_MODEL
    o_wlo = rw
    o_alo = o_wlo + DECAY_LORA
    o_glo = o_alo + ICLR_LORA
    o_m = o_glo + GATE_LORA
    o_xbc = o_m + MAMBA_INNER
    o_dt = o_xbc + MAMBA_INNER + BC_WIDTH
    o_gate = o_dt + MAMBA_HEADS
    d = w_in_l.shape[0]
    vlo = w_v1_l if w_v1_l is not None else jnp.zeros((d, VALUE_LORA), w_in_l.dtype)
    dtv = _pad_cols(jnp.concatenate([w_in_l[:, o_dt:o_dt + MAMBA_HEADS], vlo], axis=1), LANES)
    cols = [
        w_in_l[:, o_m:o_m + MAMBA_INNER],
        w_in_l[:, o_gate:o_gate + 2 * D_MODEL],
        w_in_l[:, o_xbc:o_xbc + MAMBA_INNER],
        w_in_l[:, 0:rw],
        w_in_l[:, o_xbc + MAMBA_INNER:o_xbc + MAMBA_INNER + BC_WIDTH],
        w_in_l[:, o_wlo:o_glo],
        w_in_l[:, o_glo:o_m],
        dtv,
        jnp.zeros((d, N_COLS - COL_DTV - LANES), w_in_l.dtype),
    ]
    return jnp.concatenate(cols, axis=1).astype(BF16)


def _pack_rwkv_params(l, mu_rwkv, w0, a0, v0, k_k, k_a, r_k, lnx_g, lnx_b, w_decay2, w_iclr2, w_gate2,
                      mu_v, w_v2):
    mu = mu_rwkv[l]
    has_vres = l > 0
    zeros = jnp.zeros((D_MODEL,), F32)
    rows = [mu[0:D_MODEL], mu[D_MODEL:2 * D_MODEL], mu[2 * D_MODEL:3 * D_MODEL], w0[l], a0[l],
            v0[l - 1] if has_vres else zeros, k_k[l], k_a[l], r_k[l].reshape(-1), lnx_g[l], lnx_b[l]]
    rows += [zeros] * (RP_ROWS - len(rows))
    rowp = jnp.stack(rows, axis=0)
    rw = 3 * D_MODEL
    mu_vl = jnp.zeros((LANES,), F32)
    if has_vres:
        mu_vl = mu_vl.at[VLO_LANE:VLO_LANE + VALUE_LORA].set(mu_v[l - 1])
    smallp = jnp.stack([mu[rw:rw + LANES], mu[rw + LANES:rw + 2 * LANES], mu_vl]
                       + [jnp.zeros((LANES,), F32)] * (SUBLANES - 3), axis=0)
    z = lambda n: jnp.zeros((n, D_MODEL), F32)
    wd = jnp.concatenate([w_decay2[l], z(LANES - DECAY_LORA)], axis=0)
    wi = jnp.concatenate([z(DECAY_LORA), w_iclr2[l]], axis=0)
    wg = w_gate2[l]
    if has_vres:
        wv = jnp.concatenate([z(VLO_LANE), w_v2[l - 1], z(LANES - VLO_LANE - VALUE_LORA)], axis=0)
    else:
        wv = z(LANES)
    lora = jnp.stack([wd, wi, wg, wv], axis=0)
    return rowp, smallp, lora.astype(BF16)


def _pack_mamba_params(l, conv_w, conv_b, dt_bias, a_log, d_skip, mnorm_g):
    cw = conv_w[l].astype(F32)
    cb = conv_b[l].reshape(1, -1)
    cw_xs = jnp.concatenate([cw[:, :MAMBA_INNER], jnp.zeros((SUBLANES - MAMBA_CONV, MAMBA_INNER), F32)], axis=0)
    cw_bc = jnp.concatenate([cw[:, MAMBA_INNER:], jnp.zeros((SUBLANES - MAMBA_CONV, BC_WIDTH), F32)], axis=0)
    pad = lambda v: jnp.pad(v, (0, LANES - MAMBA_HEADS))
    dtp = jnp.stack([pad(dt_bias[l]), pad(-jnp.exp(a_log[l].astype(F32)))]
                    + [jnp.zeros((LANES,), F32)] * (SUBLANES - 2), axis=0)
    dskip_x = jnp.repeat(d_skip[l], MAMBA_HEAD).reshape(1, MAMBA_INNER)
    expand = (jnp.arange(LANES)[:, None] == (jnp.arange(MAMBA_INNER)[None, :] // MAMBA_HEAD)).astype(BF16)
    return cw_xs, cb[:, :MAMBA_INNER], cw_bc, cb[:, MAMBA_INNER:], dtp, dskip_x, mnorm_g[l].reshape(1, -1), expand


def _route_positions(ri, cnt, n_blocks):
    counts = cnt[0, :N_EXPERTS].astype(jnp.int32)
    pcounts = (counts + EXPERT_ROWS - 1) // EXPERT_ROWS * EXPERT_ROWS
    pends = jnp.cumsum(pcounts)
    poffsets = pends - pcounts
    eid = ri[:, RI_E1:RI_E2 + 1]
    onehot = eid[:, :, None] == jnp.arange(N_EXPERTS, dtype=jnp.int32)
    pos = jnp.sum(jnp.where(onehot, poffsets, 0), axis=-1) + ri[:, RI_RANK1:RI_RANK2 + 1]
    n_used = pends[N_EXPERTS - 1:] // EXPERT_ROWS
    meta = jnp.concatenate([pends, pcounts, n_used]).astype(jnp.int32)
    block_start = jnp.arange(n_blocks, dtype=jnp.int32) * EXPERT_ROWS
    block_e = jnp.minimum(jnp.sum(block_start[:, None] >= pends[None, :], axis=1), N_EXPERTS - 1)
    block_e = jnp.concatenate([block_e.astype(jnp.int32), n_used.astype(jnp.int32)])
    return pos, meta, block_e


def kernel(x, norm1_g, w_in, mu_rwkv, w0, w_decay2, a0, w_iclr2, w_gate2, k_k, k_a, r_k, lnx_g, lnx_b, w_rwkv_o, w_v1, mu_v, v0, w_v2, conv_w, conv_b, dt_bias, a_log, d_skip, mnorm_g, w_mamba_o, w_out, norm2_g, w_rg, b_rg, w_re, b_re, w_e1, w_e3, w_e2, final_g):
    bsz, seq, d = x.shape
    t = bsz * seq
    depth = w_in.shape[0]
    n_assign = 2 * t
    n_blocks = -(-(n_assign + N_EXPERTS * (EXPERT_ROWS - 1)) // EXPERT_ROWS)
    n_rows = n_blocks * EXPERT_ROWS
    tm_proj = min(INPROJ_TM, t)
    tm_merge = min(MERGE_TM, t)
    tm_moe = min(MOE_TM, t)

    x2d = x.reshape(t, d)
    v_first = None
    for l in range(depth):
        tm_disp = min(2 * MOE_TM, t) if l == 0 else tm_moe
        w_cat = _pack_w_in(w_in[l], w_v1[l - 1] if l > 0 else None)
        proj = _inproj(x2d, norm1_g[l].reshape(1, d), w_cat, tm_proj, INPROJ_TN)
        proj3 = proj.reshape(bsz, seq, N_COLS)
        rowp, smallp, lora = _pack_rwkv_params(l, mu_rwkv, w0, a0, v0, k_k, k_a, r_k, lnx_g, lnx_b,
                                               w_decay2, w_iclr2, w_gate2, mu_v, w_v2)
        if l == 0:
            ya, v_first = _rwkv_branch(proj3, None, rowp, smallp, lora, 2 * WKV_BATCH)
        else:
            ya = _rwkv_branch(proj3, v_first, rowp, smallp, lora, WKV_BATCH)
        yb = _mamba_branch(proj3, *_pack_mamba_params(l, conv_w, conv_b, dt_bias, a_log, d_skip, mnorm_g))

        w_r = _pad_cols(jnp.concatenate([w_rg[l], w_re[l]], axis=1), LANES).astype(BF16)
        b_r = _pad_cols(jnp.concatenate([b_rg[l], b_re[l]]).reshape(1, -1), LANES)
        x2d, h2, ri, rw, cnt = _merge(
            ya.reshape(t, D_MODEL), yb.reshape(t, MAMBA_INNER), proj, x2d,
            w_rwkv_o[l].astype(BF16), w_mamba_o[l].astype(BF16), w_out[l].astype(BF16),
            norm2_g[l].reshape(1, d), w_r, b_r, tm_merge)

        pos, meta, block_e = _route_positions(ri, cnt, n_blocks)
        xb = _dispatch(meta, pos.reshape(t // tm_disp, 2 * tm_disp), h2, n_rows, tm_disp)
        ybuf = _expert_ffn(block_e, xb, w_e1, w_e3, w_e2, l)
        x2d = _combine(pos.reshape(t // tm_moe, 2 * tm_moe), ybuf, x2d, rw, final_g.reshape(1, d),
                       l == depth - 1, tm_moe)
    return x2d.reshape(bsz, seq, d)
```

```python
import functools

import jax
import jax.numpy as jnp
from jax import lax
from jax.experimental import pallas as pl
from jax.experimental.pallas import tpu as pltpu

F32 = jnp.float32
BF16 = jnp.bfloat16

D_MODEL = 1024
RWKV_HEAD = 64
DECAY_LORA = 64
ICLR_LORA = 64
VALUE_LORA = 32
GATE_LORA = 128
RWKV_GN_EPS = 64e-5
MAMBA_INNER = 2048
MAMBA_HEAD = 64
MAMBA_HEADS = 32
MAMBA_GROUPS = 4
MAMBA_STATE = 128
MAMBA_CONV = 4
SSD_CHUNK = 128
N_EXPERT_GROUPS = 4
EXPERTS_PER_GROUP = 8
N_EXPERTS = 32
EXPERT_FF = 512
NORM_EPS = 1e-6
MAMBA_NORM_EPS = 1e-5

LANES = 128
SUBLANES = 8
D_TILES = D_MODEL // LANES
N_PAIRS = D_MODEL // LANES
WKV_CHUNK = 64
WKV_BATCH = 4
EXPERT_ROWS = 512

COL_Z = 0
COL_GATE = 2048
COL_XS = 4096
COL_R = 6144
COL_K = 7168
COL_V = 8192
COL_BC = 9216
COL_WA = 10240
COL_GLO = 10368
COL_DTV = 10496
N_COLS = 10752
VLO_LANE = 32

INPROJ_TM = 1024
INPROJ_TN = N_COLS // 7
MERGE_TM = 512
MOE_TM = 1024

V7X_VMEM_BYTES = 64 * 1024 * 1024
VMEM_LIMIT = V7X_VMEM_BYTES * 7 // 8


def _softplus(x):
    return jnp.maximum(x, 0.0) + jnp.log(1.0 + jnp.exp(-jnp.abs(x)))


def _sigmoid(x):
    return 1.0 / (1.0 + jnp.exp(-x))


def _dot(a, b):
    return jnp.dot(a, b, preferred_element_type=F32)


def _dot_nt(a, b):
    return lax.dot_general(a, b, (((1,), (1,)), ((), ())), preferred_element_type=F32)


def _split_hi_lo(x):
    hi = x.astype(BF16)
    lo = (x - hi.astype(F32)).astype(BF16)
    return hi, lo


def _inproj_kernel(x_ref, g_ref, w_ref, o_ref, h_scr):
    @pl.when(pl.program_id(1) == 0)
    def _():
        x = x_ref[...]
        ms = jnp.mean(x * x, axis=-1, keepdims=True)
        h_scr[...] = (x * lax.rsqrt(ms + NORM_EPS) * g_ref[...]).astype(BF16)

    o_ref[...] = _dot(h_scr[...], w_ref[...]).astype(o_ref.dtype)


def _inproj(x2d, g_row, w_bf16, tm, tn):
    t, d = x2d.shape
    nc = w_bf16.shape[1]
    return pl.pallas_call(
        _inproj_kernel,
        out_shape=jax.ShapeDtypeStruct((t, nc), BF16),
        grid=(t // tm, nc // tn),
        in_specs=[
            pl.BlockSpec((tm, d), lambda i, j: (i, 0)),
            pl.BlockSpec((1, d), lambda i, j: (0, 0)),
            pl.BlockSpec((d, tn), lambda i, j: (0, j)),
        ],
        out_specs=pl.BlockSpec((tm, tn), lambda i, j: (i, j)),
        scratch_shapes=[pltpu.VMEM((tm, d), BF16)],
        compiler_params=pltpu.CompilerParams(
            dimension_semantics=("arbitrary", "arbitrary"), vmem_limit_bytes=VMEM_LIMIT),
        name="inproj",
    )(x2d, g_row, w_bf16)


RP_MU_R, RP_MU_K, RP_MU_V, RP_W0, RP_A0, RP_V0, RP_KK, RP_KA, RP_RK, RP_LNG, RP_LNB = range(11)
RP_ROWS = 16
SP_MU_WA, SP_MU_G, SP_MU_VL = range(3)
LR_DECAY, LR_ICLR, LR_GATE, LR_VAL = range(4)


def _bd_stack(x, m0):
    zero = jnp.zeros_like(x)
    return jnp.concatenate([jnp.where(m0, x, zero), jnp.where(m0, zero, x)], axis=0)


def _rwkv_kernel(has_vres, nb, *refs):
    if has_vres:
        (r_ref, k_ref, v_ref, wa_ref, gl_ref, dtv_ref, vf_ref, rowp_ref, smallp_ref, lora_ref,
         y_ref, c_rkv, c_small, state, s_r, s_k, s_v, s_lw, s_wc, s_a, s_g) = refs
        vout_ref = None
    else:
        (r_ref, k_ref, v_ref, wa_ref, gl_ref, rowp_ref, smallp_ref, lora_ref,
         y_ref, vout_ref, c_rkv, c_small, state, s_r, s_k, s_v, s_lw, s_wc, s_a, s_g) = refs
        dtv_ref = vf_ref = None
    L = WKV_CHUNK
    c = pl.program_id(1)
    first = c == 0
    row = lax.broadcasted_iota(jnp.int32, (L, LANES), 0)
    lane = lax.broadcasted_iota(jnp.int32, (L, LANES), 1)

    @pl.when(first)
    def _():
        state[...] = jnp.zeros_like(state)
        c_rkv[...] = jnp.zeros_like(c_rkv)
        c_small[...] = jnp.zeros_like(c_small)

    def shift_mix(src, bb, carry_ref, idx, mu):
        u = src[bb].astype(F32)
        row0 = lax.broadcasted_iota(jnp.int32, u.shape, 0) == 0
        prev = jnp.where(row0, carry_ref[bb, idx, 0:1], pltpu.roll(u, 1, 0))
        carry_ref[bb, idx, 0:1] = u[L - 1:L]
        return u + (prev - u) * mu

    rowf = lambda slot: rowp_ref[slot:slot + 1]
    tril = (lax.broadcasted_iota(jnp.int32, (L, L), 0)
            >= lax.broadcasted_iota(jnp.int32, (L, L), 1)).astype(BF16)
    for bb in range(nb):
        s_r[bb] = shift_mix(r_ref, bb, c_rkv, 0, rowf(RP_MU_R))
        s_k[bb] = shift_mix(k_ref, bb, c_rkv, 1, rowf(RP_MU_K))
        xv = shift_mix(v_ref, bb, c_rkv, 2, rowf(RP_MU_V))
        xwa = shift_mix(wa_ref, bb, c_small, 0, smallp_ref[SP_MU_WA:SP_MU_WA + 1])
        xg = shift_mix(gl_ref, bb, c_small, 1, smallp_ref[SP_MU_G:SP_MU_G + 1])
        if has_vres:
            xvl = shift_mix(dtv_ref, bb, c_small, 2, smallp_ref[SP_MU_VL:SP_MU_VL + 1])
            mix = _sigmoid(rowf(RP_V0) + _dot(xvl.astype(BF16), lora_ref[LR_VAL]))
            xv = xv + (vf_ref[bb].astype(F32) - xv) * mix
        else:
            vout_ref[bb] = xv.astype(vout_ref.dtype)
        s_v[bb] = xv
        w_in = rowf(RP_W0) + _dot(jnp.tanh(xwa).astype(BF16), lora_ref[LR_DECAY])
        lw_all = -LOG2_E * jnp.exp(-_softplus(-w_in) - 0.5)
        s_lw[bb] = lw_all
        lw_hi, lw_lo = _split_hi_lo(lw_all)
        s_wc[bb] = _dot(tril, lw_hi) + _dot(tril, lw_lo)
        s_a[bb] = _sigmoid(rowf(RP_A0) + _dot(xwa.astype(BF16), lora_ref[LR_ICLR]))
        s_g[bb] = _dot(_sigmoid(xg).astype(BF16), lora_ref[LR_GATE])

    m0 = lane < RWKV_HEAD
    s_in = lane % RWKV_HEAD
    strict = s_in < row
    incl = s_in <= row
    lane2 = lax.broadcasted_iota(jnp.int32, (2 * L, LANES), 1)
    row2 = lax.broadcasted_iota(jnp.int32, (2 * L, LANES), 0)
    bd_mask = (lane2 // RWKV_HEAD) == (row2 // RWKV_HEAD)
    bd_ones = bd_mask.astype(BF16)
    lane4 = lax.broadcasted_iota(jnp.int32, (4 * L, 2 * LANES), 1)
    row4 = lax.broadcasted_iota(jnp.int32, (4 * L, 2 * LANES), 0)
    bd_ones2 = ((lane4 // RWKV_HEAD) == (row4 // RWKV_HEAD)).astype(BF16)
    HB = L // 2
    lane_h = lax.broadcasted_iota(jnp.int32, (HB, LANES), 1)
    row_h = lax.broadcasted_iota(jnp.int32, (HB, LANES), 0)
    second_half = (lane_h % RWKV_HEAD) >= HB
    m0_h = lane_h < RWKV_HEAD
    eye_h = ((lane_h % HB) == row_h).astype(F32)
    blk_h = lane_h // HB
    zeros_h = jnp.zeros((HB, LANES), F32)

    def bd4(xh):
        return jnp.concatenate([jnp.where(blk_h == b, xh, 0.0) for b in range(4)], axis=0).astype(BF16)

    def rows_for_half(xh, half):
        parts = [jnp.where(m0_h, xh, 0.0), zeros_h, jnp.where(m0_h, 0.0, xh), zeros_h]
        if half == 1:
            parts = [zeros_h, parts[0], zeros_h, parts[2]]
        return jnp.concatenate(parts, axis=0).astype(BF16)

    def pair_steps(bb, p):
        sl = slice(p * LANES, (p + 1) * LANES)
        rp = lambda slot: rowp_ref[slot:slot + 1, sl]
        r = s_r[bb, :, sl]
        xk = s_k[bb, :, sl]
        v = s_v[bb, :, sl]
        lw = s_lw[bb, :, sl]
        w_cum = s_wc[bb, :, sl]
        a = s_a[bb, :, sl]
        kkr = xk * rp(RP_KK)
        k2 = xk * (1.0 + (a - 1.0) * rp(RP_KA))
        sums = _dot(jnp.concatenate([kkr * kkr, r * k2 * rp(RP_RK)], axis=1).astype(BF16), bd_ones2)
        yield
        kkn = kkr / jnp.maximum(jnp.sqrt(sums[:, 0:LANES]), 1e-12)
        bonus_s = sums[:, LANES:2 * LANES]
        avec = -kkn
        bvec = kkn * a
        w_mid = w_cum[L // 2 - 1:L // 2]
        w_end = w_cum[L - 1:L]
        e_abs = jnp.exp2(w_cum)
        e_prev = jnp.exp2(w_cum - lw)
        e_from_mid = jnp.exp2(w_mid - w_cum)
        e_to_end = jnp.exp2(w_end - w_cum)
        e_mid_inv = jnp.exp2(-w_mid)
        r_abs = r * e_abs
        a_abs = avec * e_prev
        r_mid = r_abs * e_mid_inv
        a_mid = a_abs * e_mid_inv
        b_mid = bvec * e_from_mid
        k_mid = k2 * e_from_mid
        b_end = bvec * e_to_end
        k_end = k2 * e_to_end

        st = state[bb * N_PAIRS + p]
        lhs_abs = jnp.concatenate([a_abs, r_abs], axis=0).astype(BF16)
        m1 = _dot_nt(lhs_abs, st.astype(BF16))
        lhs_mid = jnp.concatenate([a_mid, r_mid], axis=0).astype(BF16)
        rhs_mid = jnp.concatenate([_bd_stack(b_mid, m0), _bd_stack(k_mid, m0)], axis=0).astype(BF16)
        m2 = _dot_nt(lhs_mid, rhs_mid)
        yield
        a_ab = jnp.where(strict, m2[0:L, 0:LANES], 0.0)
        a_ak = jnp.where(strict, m2[0:L, LANES:2 * LANES], 0.0)
        a_rb = jnp.where(incl, m2[L:2 * L, 0:LANES], 0.0)
        a_rk = jnp.where(incl, m2[L:2 * L, LANES:2 * LANES], 0.0)

        v_bd = _bd_stack(v, m0).astype(BF16)
        rhs_u = m1[0:L] + _dot(a_ak.astype(BF16), v_bd)

        top = a_ab[0:HB]
        bot = a_ab[HB:L]
        diag = jnp.where(second_half, bot, top)
        a21 = jnp.where(second_half, 0.0, bot)
        x_inv = eye_h + diag
        pw = _dot(diag.astype(BF16), bd4(diag))
        yield
        for _ in range(3):
            res = _dot(jnp.concatenate([x_inv, pw], axis=0).astype(BF16), bd4(pw))
            yield
            x_inv = x_inv + res[0:HB]
            pw = res[HB:L]
        x_upd = _dot(x_inv.astype(BF16), bd4(pw))
        yield
        x_inv_bf = (x_inv + x_upd).astype(BF16)
        u1 = _dot(x_inv_bf, rows_for_half(rhs_u[0:HB], 0))
        yield
        rhs2 = rhs_u[HB:L] + _dot(a21.astype(BF16), rows_for_half(u1, 0))
        yield
        u2 = _dot(x_inv_bf, rows_for_half(rhs2, 1))
        yield
        u = jnp.concatenate([u1, u2], axis=0)
        u_bd = _bd_stack(u, m0).astype(BF16)
        y = m1[L:2 * L] + _dot(jnp.concatenate([a_rb, a_rk], axis=1).astype(BF16),
                               jnp.concatenate([u_bd, v_bd], axis=0))

        uv_t = jnp.concatenate([u, v], axis=0).T.astype(BF16)
        upd = _dot(uv_t, jnp.concatenate([b_end, k_end], axis=0).astype(BF16))
        yield
        state[bb * N_PAIRS + p] = st * jnp.exp2(w_end) + jnp.where(bd_mask, upd, 0.0)

        inv_n = 1.0 / RWKV_HEAD
        mean = _dot(y.astype(BF16), bd_ones) * inv_n
        yield
        yc = y - mean
        var = _dot((yc * yc).astype(BF16), bd_ones) * inv_n
        yield
        yn = yc * lax.rsqrt(var + RWKV_GN_EPS) * rp(RP_LNG) + rp(RP_LNB)
        y_ref[bb, :, sl] = ((yn + bonus_s * v) * s_g[bb, :, sl]).astype(y_ref.dtype)

    chains = [pair_steps(bb, p) for bb in range(nb) for p in range(N_PAIRS)]
    while chains:
        alive = []
        for chain in chains:
            try:
                next(chain)
                alive.append(chain)
            except StopIteration:
                pass
        chains = alive


def _rwkv_branch(proj3, v_first, rowp, smallp, lora, nb):
    b, s, _ = proj3.shape
    L = WKV_CHUNK
    has_vres = v_first is not None
    assert b % nb == 0 and s % L == 0, (b, s)
    wide = lambda col: pl.BlockSpec((nb, L, D_MODEL), lambda i, j, c=col // D_MODEL: (i, j, c))
    narrow = lambda col: pl.BlockSpec((nb, L, LANES), lambda i, j, c=col // LANES: (i, j, c))
    full = lambda shape: pl.BlockSpec(shape, lambda i, j: (0,) * len(shape))
    out_blk = pl.BlockSpec((nb, L, D_MODEL), lambda i, j: (i, j, 0))
    in_specs = [wide(COL_R), wide(COL_K), wide(COL_V), narrow(COL_WA), narrow(COL_GLO)]
    args = [proj3] * 5
    if has_vres:
        in_specs += [narrow(COL_DTV), out_blk]
        args += [proj3, v_first]
    in_specs += [full(rowp.shape), full(smallp.shape), full(lora.shape)]
    args += [rowp, smallp, lora]
    wide_scr = lambda: pltpu.VMEM((nb, L, D_MODEL), F32)
    scratch = [pltpu.VMEM((nb, 3, SUBLANES, D_MODEL), F32), pltpu.VMEM((nb, 3, SUBLANES, LANES), F32),
               pltpu.VMEM((nb * N_PAIRS, 2 * RWKV_HEAD, LANES), F32)] + [wide_scr() for _ in range(7)]
    y_shape = jax.ShapeDtypeStruct((b, s, D_MODEL), BF16)
    out_shape = y_shape if has_vres else (y_shape, y_shape)
    out_specs = out_blk if has_vres else (out_blk, out_blk)
    return pl.pallas_call(
        functools.partial(_rwkv_kernel, has_vres, nb),
        out_shape=out_shape,
        grid=(b // nb, s // L),
        in_specs=in_specs,
        out_specs=out_specs,
        scratch_shapes=scratch,
        compiler_params=pltpu.CompilerParams(
            dimension_semantics=("arbitrary", "arbitrary"), vmem_limit_bytes=VMEM_LIMIT),
        name="rwkv7",
    )(*args)


HEADS_PER_GROUP = MAMBA_HEADS // MAMBA_GROUPS
GROUP_WIDTH = MAMBA_INNER // MAMBA_GROUPS
BC_WIDTH = 2 * MAMBA_GROUPS * MAMBA_STATE
DP_DT_BIAS, DP_A_HEAD = range(2)
LOG2_E = 1.4426950408889634


def _silu(x):
    half = 0.5 * x
    return half + half * jnp.tanh(half)


def _mamba_kernel(z_ref, xs_ref, bc_ref, dtv_ref, cw_xs_ref, cb_xs_ref, cw_bc_ref, cb_bc_ref, dtp_ref,
                  dskip_ref, mnorm_ref, expand_ref, y_ref, tail_xs, tail_bc, state):
    L = SSD_CHUNK
    c = pl.program_id(1)
    first = c == 0

    @pl.when(first)
    def _():
        state[...] = jnp.zeros_like(state)
        tail_xs[...] = jnp.zeros_like(tail_xs)
        tail_bc[...] = jnp.zeros_like(tail_bc)

    def conv_silu(src_ref, tail_ref, w_ref, b_ref, cols):
        u = src_ref[0, :, cols].astype(F32)
        row8 = lax.broadcasted_iota(jnp.int32, (SUBLANES, u.shape[1]), 0)
        tail = tail_ref[:, cols]
        acc = u * w_ref[MAMBA_CONV - 1:MAMBA_CONV, cols] + b_ref[:, cols]
        for d in range(1, MAMBA_CONV):
            rolled = pltpu.roll(u, d, 0)
            head = jnp.where(row8 < d, pltpu.roll(tail, d, 0), rolled[0:SUBLANES])
            shifted = jnp.concatenate([head, rolled[SUBLANES:]], axis=0)
            acc = acc + shifted * w_ref[MAMBA_CONV - 1 - d:MAMBA_CONV - d, cols]
        tail_ref[:, cols] = u[L - SUBLANES:L]
        return _silu(acc)

    dt = _softplus(dtv_ref[0].astype(F32) + dtp_ref[DP_DT_BIAS:DP_DT_BIAS + 1])
    adt = dt * dtp_ref[DP_A_HEAD:DP_A_HEAD + 1]
    tril_b = lax.broadcasted_iota(jnp.int32, (L, L), 0) >= lax.broadcasted_iota(jnp.int32, (L, L), 1)
    tril = tril_b.astype(BF16)
    adt_hi, adt_lo = _split_hi_lo(adt)
    a_cs = _dot(tril, adt_hi) + _dot(tril, adt_lo)
    a_cs2 = a_cs * LOG2_E
    a_cs2_t = a_cs2.T
    dt_bf = dt.astype(BF16)
    exp_acs_bf = jnp.exp(a_cs).astype(BF16)
    to_end_bf = jnp.exp(a_cs[L - 1:L] - a_cs).astype(BF16)
    lane = lax.broadcasted_iota(jnp.int32, (L, LANES), 1)
    m0 = lane < MAMBA_HEAD

    for g in range(MAMBA_GROUPS):
        gs = slice(g * GROUP_WIDTH, (g + 1) * GROUP_WIDTH)
        b_cols = slice(g * MAMBA_STATE, (g + 1) * MAMBA_STATE)
        c_cols = slice((MAMBA_GROUPS + g) * MAMBA_STATE, (MAMBA_GROUPS + g + 1) * MAMBA_STATE)
        xc = conv_silu(xs_ref, tail_xs, cw_xs_ref, cb_xs_ref, gs)
        bg = conv_silu(bc_ref, tail_bc, cw_bc_ref, cb_bc_ref, b_cols)
        cg_bf = conv_silu(bc_ref, tail_bc, cw_bc_ref, cb_bc_ref, c_cols).astype(BF16)
        expand = expand_ref[:, gs]
        exp_acs_x = _dot(exp_acs_bf, expand)
        xdt = xc * _dot(dt_bf, expand)
        xdt_bf = xdt.astype(BF16)
        xw_bf = (xdt * _dot(to_end_bf, expand)).astype(BF16)
        cb = jnp.where(tril_b, _dot_nt(cg_bf, bg.astype(BF16)), 0.0)
        st = state[g]
        y_off = _dot(cg_bf, st.astype(BF16)) * exp_acs_x
        y_parts = []
        for jp in range(HEADS_PER_GROUP // 2):
            ws = []
            for hh in range(2):
                j = g * HEADS_PER_GROUP + 2 * jp + hh
                seg2 = a_cs2[:, j:j + 1] - a_cs2_t[j:j + 1, :]
                ws.append((cb * jnp.exp2(jnp.minimum(seg2, 0.0))).astype(BF16))
            ps = slice(jp * LANES, (jp + 1) * LANES)
            y_parts.append(_dot(jnp.concatenate(ws, axis=1), _bd_stack(xdt_bf[:, ps], m0)))
        state[g] = st * exp_acs_x[L - 1:L] + _dot(bg.T.astype(BF16), xw_bf)
        y = jnp.concatenate(y_parts, axis=1) + y_off + xc * dskip_ref[:, gs]
        y = y * _silu(z_ref[0, :, gs].astype(F32))
        ms = jnp.mean(y * y, axis=-1, keepdims=True)
        y_ref[0, :, gs] = (y * lax.rsqrt(ms + MAMBA_NORM_EPS) * mnorm_ref[:, gs]).astype(y_ref.dtype)


def _mamba_branch(proj3, cw_xs, cb_xs, cw_bc, cb_bc, dtp, dskip_x, mnorm, expand):
    b, s, _ = proj3.shape
    L = SSD_CHUNK
    blk = lambda width, col: pl.BlockSpec((1, L, width), lambda i, j, c=col // width: (i, j, c))
    full = lambda arr: pl.BlockSpec(arr.shape, lambda i, j: (0,) * arr.ndim)
    params = [cw_xs, cb_xs, cw_bc, cb_bc, dtp, dskip_x, mnorm, expand]
    return pl.pallas_call(
        _mamba_kernel,
        out_shape=jax.ShapeDtypeStruct((b, s, MAMBA_INNER), BF16),
        grid=(b, s // L),
        in_specs=[blk(MAMBA_INNER, COL_Z), blk(MAMBA_INNER, COL_XS), blk(BC_WIDTH, COL_BC),
                  blk(LANES, COL_DTV)] + [full(a) for a in params],
        out_specs=pl.BlockSpec((1, L, MAMBA_INNER), lambda i, j: (i, j, 0)),
        scratch_shapes=[pltpu.VMEM((SUBLANES, MAMBA_INNER), F32), pltpu.VMEM((SUBLANES, BC_WIDTH), F32),
                        pltpu.VMEM((MAMBA_GROUPS, MAMBA_STATE, GROUP_WIDTH), F32)],
        compiler_params=pltpu.CompilerParams(
            dimension_semantics=("arbitrary", "arbitrary"), vmem_limit_bytes=VMEM_LIMIT),
        name="mamba2_ssd",
    )(proj3, proj3, proj3, proj3, *params)


ROUTE_E0 = N_EXPERT_GROUPS
RI_E1, RI_E2, RI_RANK1, RI_RANK2 = range(4)
BIG = 1e30


def _merge_kernel(ya_ref, yb_ref, gate_ref, x_ref, wro_ref, wmo_ref, wout_ref, g2_ref, wr_ref, br_ref,
                  xo_ref, h2_ref, ri_ref, rw_ref, cnt_ref, run_cnt):
    tm = x_ref.shape[0]
    i = pl.program_id(0)

    @pl.when(i == 0)
    def _():
        run_cnt[...] = jnp.zeros_like(run_cnt)

    o_a = _dot(ya_ref[...], wro_ref[...])
    o_b = _dot(yb_ref[...], wmo_ref[...])
    gates = _sigmoid(gate_ref[...].astype(F32))
    merged = gates[:, 0:D_MODEL] * o_a + gates[:, D_MODEL:2 * D_MODEL] * o_b
    x = x_ref[...] + _dot(merged.astype(BF16), wout_ref[...])
    xo_ref[...] = x
    ms = jnp.mean(x * x, axis=-1, keepdims=True)
    h2 = x * lax.rsqrt(ms + NORM_EPS) * g2_ref[...]
    h2_ref[...] = h2

    logits = _dot(h2.astype(BF16), wr_ref[...]) + br_ref[...]
    lane = lax.broadcasted_iota(jnp.int32, (tm, LANES), 1)
    gmask = lane < N_EXPERT_GROUPS
    gl = jnp.where(gmask, logits, -BIG)
    gmax = jnp.max(gl, axis=-1, keepdims=True)
    gidx = jnp.min(jnp.where(gl == gmax, lane, LANES), axis=-1, keepdims=True)
    gsum = jnp.sum(jnp.where(gmask, jnp.exp(gl - gmax), 0.0), axis=-1, keepdims=True)
    g_p = 1.0 / gsum
    e_lo = ROUTE_E0 + gidx * EXPERTS_PER_GROUP
    emask = (lane >= e_lo) & (lane < e_lo + EXPERTS_PER_GROUP)
    el = jnp.where(emask, logits, -BIG)
    e1 = jnp.max(el, axis=-1, keepdims=True)
    i1 = jnp.min(jnp.where(el == e1, lane, LANES), axis=-1, keepdims=True)
    el2 = jnp.where(lane == i1, -BIG, el)
    e2 = jnp.max(el2, axis=-1, keepdims=True)
    i2 = jnp.min(jnp.where(el2 == e2, lane, LANES), axis=-1, keepdims=True)
    q = jnp.exp(e2 - e1)
    w1 = g_p / (1.0 + q)
    w2 = g_p * q / (1.0 + q)
    eid1 = i1 - ROUTE_E0
    eid2 = i2 - ROUTE_E0

    onehot = ((lane == eid1) | (lane == eid2)).astype(BF16)
    strict = (lax.broadcasted_iota(jnp.int32, (tm, tm), 0)
              > lax.broadcasted_iota(jnp.int32, (tm, tm), 1)).astype(BF16)
    before = _dot(strict, onehot) + run_cnt[0:1]
    rank1 = jnp.sum(jnp.where(lane == eid1, before, 0.0), axis=-1, keepdims=True).astype(jnp.int32)
    rank2 = jnp.sum(jnp.where(lane == eid2, before, 0.0), axis=-1, keepdims=True).astype(jnp.int32)
    total = run_cnt[0:1] + jnp.sum(onehot.astype(F32), axis=0, keepdims=True)
    run_cnt[0:1] = total
    cnt_ref[...] = jnp.broadcast_to(total, cnt_ref.shape)

    ri = jnp.where(lane == RI_E1, eid1, 0)
    ri = jnp.where(lane == RI_E2, eid2, ri)
    ri = jnp.where(lane == RI_RANK1, rank1, ri)
    ri = jnp.where(lane == RI_RANK2, rank2, ri)
    ri_ref[...] = ri
    rw_ref[...] = jnp.where(lane == 0, w1, jnp.where(lane == 1, w2, 0.0))


def _merge(ya, yb, proj, x2d, w_ro, w_mo, w_out, g2_row, w_r, b_r, tm):
    t = x2d.shape[0]
    full = lambda arr: pl.BlockSpec(arr.shape, lambda i: (0,) * arr.ndim, pipeline_mode=pl.Buffered(1))
    rows = lambda width: pl.BlockSpec((tm, width), lambda i: (i, 0))
    return pl.pallas_call(
        _merge_kernel,
        out_shape=(jax.ShapeDtypeStruct((t, D_MODEL), F32),
                   jax.ShapeDtypeStruct((t, D_MODEL), F32),
                   jax.ShapeDtypeStruct((t, LANES), jnp.int32),
                   jax.ShapeDtypeStruct((t, LANES), F32),
                   jax.ShapeDtypeStruct((SUBLANES, LANES), F32)),
        grid=(t // tm,),
        in_specs=[rows(D_MODEL), rows(MAMBA_INNER),
                  pl.BlockSpec((tm, 2 * D_MODEL), lambda i: (i, COL_GATE // (2 * D_MODEL))),
                  rows(D_MODEL), full(w_ro), full(w_mo), full(w_out), full(g2_row), full(w_r), full(b_r)],
        out_specs=(rows(D_MODEL), rows(D_MODEL),
                   rows(LANES), rows(LANES), pl.BlockSpec((SUBLANES, LANES), lambda i: (0, 0))),
        scratch_shapes=[pltpu.VMEM((SUBLANES, LANES), F32)],
        compiler_params=pltpu.CompilerParams(
            dimension_semantics=("arbitrary",), vmem_limit_bytes=VMEM_LIMIT),
        name="merge_route",
    )(ya, yb, proj, x2d, w_ro, w_mo, w_out, g2_row, w_r, b_r)


def _fetch_step_indices(pos_hbm, idx_smem, idx_sem):
    i = pl.program_id(0)
    n = pl.num_programs(0)
    slot = i % 2
    per_step = pos_hbm.shape[1]

    def copy(step, s):
        dst = idx_smem.at[pl.ds(pl.multiple_of(s * per_step, per_step), per_step)]
        return pltpu.make_async_copy(pos_hbm.at[step], dst, idx_sem.at[s])

    @pl.when(i == 0)
    def _():
        copy(0, 0).start()

    copy(i, slot).wait()

    @pl.when(i + 1 < n)
    def _():
        copy(i + 1, 1 - slot).start()

    return slot * per_step


def _issue_row_copies(row_copy, n_groups):
    def issue(g, carry):
        for u in range(SUBLANES):
            for k in range(2):
                row_copy(g, u, k).start(priority=k)
        return carry

    lax.fori_loop(0, n_groups, issue, 0)


META_PEND, META_PCOUNT, META_N_USED = 0, N_EXPERTS, 2 * N_EXPERTS
META_SIZE = 2 * N_EXPERTS + 1


def _dispatch_kernel(min_used_blocks, meta_ref, pos_hbm, h2_ref, xb_out, idx_smem, zero_blk, idx_sem, row_sem,
                     zero_sem):
    n_groups = h2_ref.shape[0]
    tm = n_groups * SUBLANES
    n_blocks = xb_out.shape[0] // EXPERT_ROWS
    n_used = meta_ref[META_N_USED]

    @pl.when(pl.program_id(0) == 0)
    def _():
        zero_blk[...] = jnp.zeros_like(zero_blk)

        def zero_fill(start):
            dst = xb_out.at[pl.ds(pl.multiple_of(start, EXPERT_ROWS), EXPERT_ROWS)]
            return pltpu.make_async_copy(zero_blk, dst, zero_sem)

        fills = [(meta_ref[META_PCOUNT + e] > 0, meta_ref[META_PEND + e] - EXPERT_ROWS)
                 for e in range(N_EXPERTS)]
        fills += [(blk >= n_used, blk * EXPERT_ROWS) for blk in range(min_used_blocks, n_blocks)]
        for pred, start in fills:
            @pl.when(pred)
            def _():
                zero_fill(start).start()
        for pred, start in fills:
            @pl.when(pred)
            def _():
                zero_fill(start).wait()

    base = _fetch_step_indices(pos_hbm, idx_smem, idx_sem)

    def row_copy(g, u, k):
        row = idx_smem[base + 2 * SUBLANES * g + (2 * u + k)]
        return pltpu.make_async_copy(h2_ref.at[g, pl.ds(u, 1)], xb_out.at[pl.ds(row, 1)], row_sem)

    _issue_row_copies(row_copy, n_groups)
    for _ in range(2):
        pltpu.make_async_copy(xb_out.at[pl.ds(0, tm)], xb_out.at[pl.ds(0, tm)], row_sem).wait()


def _dispatch(meta, pos2, h2, n_rows, tm):
    t = h2.shape[0]
    min_used_blocks = (2 * t) // EXPERT_ROWS
    grid_spec = pltpu.PrefetchScalarGridSpec(
        num_scalar_prefetch=1,
        grid=(t // tm,),
        in_specs=[pl.BlockSpec(memory_space=pl.ANY),
                  pl.BlockSpec((tm // SUBLANES, SUBLANES, D_MODEL), lambda i, meta: (i, 0, 0))],
        out_specs=pl.BlockSpec(memory_space=pl.ANY),
        scratch_shapes=[pltpu.SMEM((4 * tm,), jnp.int32), pltpu.VMEM((EXPERT_ROWS, D_MODEL), F32),
                        pltpu.SemaphoreType.DMA((2,)), pltpu.SemaphoreType.DMA, pltpu.SemaphoreType.DMA],
    )
    return pl.pallas_call(
        functools.partial(_dispatch_kernel, min_used_blocks),
        out_shape=jax.ShapeDtypeStruct((n_rows, D_MODEL), F32),
        grid_spec=grid_spec,
        compiler_params=pltpu.CompilerParams(
            dimension_semantics=("arbitrary",), vmem_limit_bytes=VMEM_LIMIT),
        name="moe_dispatch",
    )(meta, pos2, h2.reshape(t // SUBLANES, SUBLANES, D_MODEL))


def _ffn_kernel(be_ref, xb_ref, w1_ref, w3_ref, w2_ref, yb_ref, w1_bf, w3_bf, w2_bf):
    i = pl.program_id(0)
    n_used = be_ref[pl.num_programs(0)]

    @pl.when((i == 0) | (be_ref[i] != be_ref[jnp.maximum(i - 1, 0)]))
    def _():
        w1_bf[...] = w1_ref[0, 0].astype(BF16)
        w3_bf[...] = w3_ref[0, 0].astype(BF16)
        w2_bf[...] = w2_ref[0, 0].astype(BF16)

    @pl.when(i < n_used)
    def _():
        x = xb_ref[...].astype(BF16)
        h1 = _dot(x, w1_bf[...])
        h3 = _dot(x, w3_bf[...])
        hid = (_silu(h1) * h3).astype(BF16)
        yb_ref[...] = _dot(hid, w2_bf[...])

    @pl.when(i >= n_used)
    def _():
        yb_ref[...] = jnp.zeros_like(yb_ref)


def _expert_ffn(block_e, xb, w1, w3, w2, layer):
    n_rows = xb.shape[0]
    br = EXPERT_ROWS
    grid_spec = pltpu.PrefetchScalarGridSpec(
        num_scalar_prefetch=1,
        grid=(n_rows // br,),
        in_specs=[pl.BlockSpec((br, D_MODEL), lambda i, be: (i, 0)),
                  pl.BlockSpec((1, 1, D_MODEL, EXPERT_FF), lambda i, be: (layer, be[i], 0, 0)),
                  pl.BlockSpec((1, 1, D_MODEL, EXPERT_FF), lambda i, be: (layer, be[i], 0, 0)),
                  pl.BlockSpec((1, 1, EXPERT_FF, D_MODEL), lambda i, be: (layer, be[i], 0, 0))],
        out_specs=pl.BlockSpec((br, D_MODEL), lambda i, be: (i, 0)),
        scratch_shapes=[pltpu.VMEM((D_MODEL, EXPERT_FF), BF16), pltpu.VMEM((D_MODEL, EXPERT_FF), BF16),
                        pltpu.VMEM((EXPERT_FF, D_MODEL), BF16)],
    )
    return pl.pallas_call(
        _ffn_kernel,
        out_shape=jax.ShapeDtypeStruct(xb.shape, F32),
        grid_spec=grid_spec,
        compiler_params=pltpu.CompilerParams(
            dimension_semantics=("arbitrary",), vmem_limit_bytes=VMEM_LIMIT),
        name="expert_ffn",
    )(block_e, xb, w1, w3, w2)


def _combine_kernel(final_norm, pos_hbm, yb_hbm, x_ref, rw_ref, fg_ref, o_ref, idx_smem, buf, idx_sem, row_sem):
    tm = x_ref.shape[0]
    n_groups = tm // SUBLANES
    base = _fetch_step_indices(pos_hbm, idx_smem, idx_sem)

    def row_copy(g, u, k):
        row = idx_smem[base + 2 * SUBLANES * g + (2 * u + k)]
        return pltpu.make_async_copy(yb_hbm.at[pl.ds(row, 1)], buf.at[k, g, pl.ds(u, 1)], row_sem)

    _issue_row_copies(row_copy, n_groups)
    for _ in range(2):
        pltpu.make_async_copy(yb_hbm.at[pl.ds(0, tm)], yb_hbm.at[pl.ds(0, tm)], row_sem).wait()

    rw = rw_ref[...]
    y1 = buf[0].reshape(tm, D_MODEL)
    y2 = buf[1].reshape(tm, D_MODEL)
    x = x_ref[...] + rw[:, 0:1] * y1 + rw[:, 1:2] * y2
    if final_norm:
        ms = jnp.mean(x * x, axis=-1, keepdims=True)
        x = x * lax.rsqrt(ms + NORM_EPS) * fg_ref[...]
    o_ref[...] = x


def _combine(pos2, yb, x2d, rw, final_g_row, final_norm, tm):
    t = x2d.shape[0]
    return pl.pallas_call(
        functools.partial(_combine_kernel, final_norm),
        out_shape=jax.ShapeDtypeStruct(x2d.shape, F32),
        grid=(t // tm,),
        in_specs=[pl.BlockSpec(memory_space=pl.ANY), pl.BlockSpec(memory_space=pl.ANY),
                  pl.BlockSpec((tm, D_MODEL), lambda i: (i, 0)),
                  pl.BlockSpec((tm, LANES), lambda i: (i, 0)),
                  pl.BlockSpec((1, D_MODEL), lambda i: (0, 0))],
        out_specs=pl.BlockSpec((tm, D_MODEL), lambda i: (i, 0)),
        scratch_shapes=[pltpu.SMEM((4 * tm,), jnp.int32),
                        pltpu.VMEM((2, tm // SUBLANES, SUBLANES, D_MODEL), F32),
                        pltpu.SemaphoreType.DMA((2,)), pltpu.SemaphoreType.DMA],
        compiler_params=pltpu.CompilerParams(
            dimension_semantics=("arbitrary",), vmem_limit_bytes=VMEM_LIMIT),
        name="moe_combine",
    )(pos2, yb, x2d, rw, final_g_row)


def _pad_cols(w, width):
    return jnp.pad(w, ((0, 0), (0, width - w.shape[1])))


def _pack_w_in(w_in_l, w_v1_l):
    rw = 3 * D_MODEL
    o_wlo = rw
    o_alo = o_wlo + DECAY_LORA
    o_glo = o_alo + ICLR_LORA
    o_m = o_glo + GATE_LORA
    o_xbc = o_m + MAMBA_INNER
    o_dt = o_xbc + MAMBA_INNER + BC_WIDTH
    o_gate = o_dt + MAMBA_HEADS
    d = w_in_l.shape[0]
    vlo = w_v1_l if w_v1_l is not None else jnp.zeros((d, VALUE_LORA), w_in_l.dtype)
    dtv = _pad_cols(jnp.concatenate([w_in_l[:, o_dt:o_dt + MAMBA_HEADS], vlo], axis=1), LANES)
    cols = [
        w_in_l[:, o_m:o_m + MAMBA_INNER],
        w_in_l[:, o_gate:o_gate + 2 * D_MODEL],
        w_in_l[:, o_xbc:o_xbc + MAMBA_INNER],
        w_in_l[:, 0:rw],
        w_in_l[:, o_xbc + MAMBA_INNER:o_xbc + MAMBA_INNER + BC_WIDTH],
        w_in_l[:, o_wlo:o_glo],
        w_in_l[:, o_glo:o_m],
        dtv,
        jnp.zeros((d, N_COLS - COL_DTV - LANES), w_in_l.dtype),
    ]
    return jnp.concatenate(cols, axis=1).astype(BF16)


def _pack_rwkv_params(l, mu_rwkv, w0, a0, v0, k_k, k_a, r_k, lnx_g, lnx_b, w_decay2, w_iclr2, w_gate2,
                      mu_v, w_v2):
    mu = mu_rwkv[l]
    has_vres = l > 0
    zeros = jnp.zeros((D_MODEL,), F32)
    rows = [mu[0:D_MODEL], mu[D_MODEL:2 * D_MODEL], mu[2 * D_MODEL:3 * D_MODEL], w0[l], a0[l],
            v0[l - 1] if has_vres else zeros, k_k[l], k_a[l], r_k[l].reshape(-1), lnx_g[l], lnx_b[l]]
    rows += [zeros] * (RP_ROWS - len(rows))
    rowp = jnp.stack(rows, axis=0)
    rw = 3 * D_MODEL
    mu_vl = jnp.zeros((LANES,), F32)
    if has_vres:
        mu_vl = mu_vl.at[VLO_LANE:VLO_LANE + VALUE_LORA].set(mu_v[l - 1])
    smallp = jnp.stack([mu[rw:rw + LANES], mu[rw + LANES:rw + 2 * LANES], mu_vl]
                       + [jnp.zeros((LANES,), F32)] * (SUBLANES - 3), axis=0)
    z = lambda n: jnp.zeros((n, D_MODEL), F32)
    wd = jnp.concatenate([w_decay2[l], z(LANES - DECAY_LORA)], axis=0)
    wi = jnp.concatenate([z(DECAY_LORA), w_iclr2[l]], axis=0)
    wg = w_gate2[l]
    if has_vres:
        wv = jnp.concatenate([z(VLO_LANE), w_v2[l - 1], z(LANES - VLO_LANE - VALUE_LORA)], axis=0)
    else:
        wv = z(LANES)
    lora = jnp.stack([wd, wi, wg, wv], axis=0)
    return rowp, smallp, lora.astype(BF16)


def _pack_mamba_params(l, conv_w, conv_b, dt_bias, a_log, d_skip, mnorm_g):
    cw = conv_w[l].astype(F32)
    cb = conv_b[l].reshape(1, -1)
    cw_xs = jnp.concatenate([cw[:, :MAMBA_INNER], jnp.zeros((SUBLANES - MAMBA_CONV, MAMBA_INNER), F32)], axis=0)
    cw_bc = jnp.concatenate([cw[:, MAMBA_INNER:], jnp.zeros((SUBLANES - MAMBA_CONV, BC_WIDTH), F32)], axis=0)
    pad = lambda v: jnp.pad(v, (0, LANES - MAMBA_HEADS))
    dtp = jnp.stack([pad(dt_bias[l]), pad(-jnp.exp(a_log[l].astype(F32)))]
                    + [jnp.zeros((LANES,), F32)] * (SUBLANES - 2), axis=0)
    dskip_x = jnp.repeat(d_skip[l], MAMBA_HEAD).reshape(1, MAMBA_INNER)
    expand = (jnp.arange(LANES)[:, None] == (jnp.arange(MAMBA_INNER)[None, :] // MAMBA_HEAD)).astype(BF16)
    return cw_xs, cb[:, :MAMBA_INNER], cw_bc, cb[:, MAMBA_INNER:], dtp, dskip_x, mnorm_g[l].reshape(1, -1), expand


def _route_positions(ri, cnt, n_blocks):
    counts = cnt[0, :N_EXPERTS].astype(jnp.int32)
    pcounts = (counts + EXPERT_ROWS - 1) // EXPERT_ROWS * EXPERT_ROWS
    pends = jnp.cumsum(pcounts)
    poffsets = pends - pcounts
    eid = ri[:, RI_E1:RI_E2 + 1]
    onehot = eid[:, :, None] == jnp.arange(N_EXPERTS, dtype=jnp.int32)
    pos = jnp.sum(jnp.where(onehot, poffsets, 0), axis=-1) + ri[:, RI_RANK1:RI_RANK2 + 1]
    n_used = pends[N_EXPERTS - 1:] // EXPERT_ROWS
    meta = jnp.concatenate([pends, pcounts, n_used]).astype(jnp.int32)
    block_start = jnp.arange(n_blocks, dtype=jnp.int32) * EXPERT_ROWS
    block_e = jnp.minimum(jnp.sum(block_start[:, None] >= pends[None, :], axis=1), N_EXPERTS - 1)
    block_e = jnp.concatenate([block_e.astype(jnp.int32), n_used.astype(jnp.int32)])
    return pos, meta, block_e


def kernel(x, norm1_g, w_in, mu_rwkv, w0, w_decay2, a0, w_iclr2, w_gate2, k_k, k_a, r_k, lnx_g, lnx_b, w_rwkv_o, w_v1, mu_v, v0, w_v2, conv_w, conv_b, dt_bias, a_log, d_skip, mnorm_g, w_mamba_o, w_out, norm2_g, w_rg, b_rg, w_re, b_re, w_e1, w_e3, w_e2, final_g):
    bsz, seq, d = x.shape
    t = bsz * seq
    depth = w_in.shape[0]
    n_assign = 2 * t
    n_blocks = -(-(n_assign + N_EXPERTS * (EXPERT_ROWS - 1)) // EXPERT_ROWS)
    n_rows = n_blocks * EXPERT_ROWS
    tm_proj = min(INPROJ_TM, t)
    tm_merge = min(MERGE_TM, t)
    tm_moe = min(MOE_TM, t)

    x2d = x.reshape(t, d)
    v_first = None
    for l in range(depth):
        w_cat = _pack_w_in(w_in[l], w_v1[l - 1] if l > 0 else None)
        proj = _inproj(x2d, norm1_g[l].reshape(1, d), w_cat, tm_proj, INPROJ_TN)
        proj3 = proj.reshape(bsz, seq, N_COLS)
        rowp, smallp, lora = _pack_rwkv_params(l, mu_rwkv, w0, a0, v0, k_k, k_a, r_k, lnx_g, lnx_b,
                                               w_decay2, w_iclr2, w_gate2, mu_v, w_v2)
        if l == 0:
            ya, v_first = _rwkv_branch(proj3, None, rowp, smallp, lora, WKV_BATCH)
        else:
            ya = _rwkv_branch(proj3, v_first, rowp, smallp, lora, WKV_BATCH)
        yb = _mamba_branch(proj3, *_pack_mamba_params(l, conv_w, conv_b, dt_bias, a_log, d_skip, mnorm_g))

        w_r = _pad_cols(jnp.concatenate([w_rg[l], w_re[l]], axis=1), LANES).astype(BF16)
        b_r = _pad_cols(jnp.concatenate([b_rg[l], b_re[l]]).reshape(1, -1), LANES)
        x2d, h2, ri, rw, cnt = _merge(
            ya.reshape(t, D_MODEL), yb.reshape(t, MAMBA_INNER), proj, x2d,
            w_rwkv_o[l].astype(BF16), w_mamba_o[l].astype(BF16), w_out[l].astype(BF16),
            norm2_g[l].reshape(1, d), w_r, b_r, tm_merge)

        pos, meta, block_e = _route_positions(ri, cnt, n_blocks)
        pos2 = pos.reshape(t // tm_moe, 2 * tm_moe)
        xb = _dispatch(meta, pos2, h2, n_rows, tm_moe)
        ybuf = _expert_ffn(block_e, xb, w_e1, w_e3, w_e2, l)
        x2d = _combine(pos2, ybuf, x2d, rw, final_g.reshape(1, d), l == depth - 1, tm_moe)
    return x2d.reshape(bsz, seq, d)
```

```python
import functools

import jax
import jax.numpy as jnp
from jax import lax
from jax.experimental import pallas as pl
from jax.experimental.pallas import tpu as pltpu

F32 = jnp.float32
BF16 = jnp.bfloat16

D_MODEL = 1024
RWKV_HEAD = 64
DECAY_LORA = 64
ICLR_LORA = 64
VALUE_LORA = 32
GATE_LORA = 128
RWKV_GN_EPS = 64e-5
MAMBA_INNER = 2048
MAMBA_HEAD = 64
MAMBA_HEADS = 32
MAMBA_GROUPS = 4
MAMBA_STATE = 128
MAMBA_CONV = 4
SSD_CHUNK = 128
N_EXPERT_GROUPS = 4
EXPERTS_PER_GROUP = 8
N_EXPERTS = 32
EXPERT_FF = 512
NORM_EPS = 1e-6
MAMBA_NORM_EPS = 1e-5

LANES = 128
SUBLANES = 8
D_TILES = D_MODEL // LANES
N_PAIRS = D_MODEL // LANES
WKV_CHUNK = 64
WKV_BATCH = 4
EXPERT_ROWS = 512

COL_Z = 0
COL_GATE = 2048
COL_XS = 4096
COL_R = 6144
COL_K = 7168
COL_V = 8192
COL_BC = 9216
COL_WA = 10240
COL_GLO = 10368
COL_DTV = 10496
N_COLS = 10752
VLO_LANE = 32

INPROJ_TM = 1024
INPROJ_TN = N_COLS // 7
MERGE_TM = 512
MOE_TM = 1024

V7X_VMEM_BYTES = 64 * 1024 * 1024
VMEM_LIMIT = V7X_VMEM_BYTES * 7 // 8


def _softplus(x):
    return jnp.maximum(x, 0.0) + jnp.log(1.0 + jnp.exp(-jnp.abs(x)))


def _sigmoid(x):
    return 1.0 / (1.0 + jnp.exp(-x))


def _dot(a, b):
    return jnp.dot(a, b, preferred_element_type=F32)


def _dot_nt(a, b):
    return lax.dot_general(a, b, (((1,), (1,)), ((), ())), preferred_element_type=F32)


def _split_hi_lo(x):
    hi = x.astype(BF16)
    lo = (x - hi.astype(F32)).astype(BF16)
    return hi, lo


def _inproj_kernel(x_ref, g_ref, w_ref, o_ref, h_scr):
    @pl.when(pl.program_id(1) == 0)
    def _():
        x = x_ref[...]
        ms = jnp.mean(x * x, axis=-1, keepdims=True)
        h_scr[...] = (x * lax.rsqrt(ms + NORM_EPS) * g_ref[...]).astype(BF16)

    o_ref[...] = _dot(h_scr[...], w_ref[...]).astype(o_ref.dtype)


def _inproj(x2d, g_row, w_bf16, tm, tn):
    t, d = x2d.shape
    nc = w_bf16.shape[1]
    return pl.pallas_call(
        _inproj_kernel,
        out_shape=jax.ShapeDtypeStruct((t, nc), BF16),
        grid=(t // tm, nc // tn),
        in_specs=[
            pl.BlockSpec((tm, d), lambda i, j: (i, 0)),
            pl.BlockSpec((1, d), lambda i, j: (0, 0)),
            pl.BlockSpec((d, tn), lambda i, j: (0, j)),
        ],
        out_specs=pl.BlockSpec((tm, tn), lambda i, j: (i, j)),
        scratch_shapes=[pltpu.VMEM((tm, d), BF16)],
        compiler_params=pltpu.CompilerParams(
            dimension_semantics=("arbitrary", "arbitrary"), vmem_limit_bytes=VMEM_LIMIT),
        name="inproj",
    )(x2d, g_row, w_bf16)


RP_MU_R, RP_MU_K, RP_MU_V, RP_W0, RP_A0, RP_V0, RP_KK, RP_KA, RP_RK, RP_LNG, RP_LNB = range(11)
RP_ROWS = 16
SP_MU_WA, SP_MU_G, SP_MU_VL = range(3)
LR_DECAY, LR_ICLR, LR_GATE, LR_VAL = range(4)


def _bd_stack(x, m0):
    zero = jnp.zeros_like(x)
    return jnp.concatenate([jnp.where(m0, x, zero), jnp.where(m0, zero, x)], axis=0)


def _rwkv_kernel(has_vres, nb, *refs):
    if has_vres:
        (r_ref, k_ref, v_ref, wa_ref, gl_ref, dtv_ref, vf_ref, rowp_ref, smallp_ref, lora_ref,
         y_ref, c_rkv, c_small, state, s_r, s_k, s_v, s_lw, s_wc, s_a, s_g) = refs
        vout_ref = None
    else:
        (r_ref, k_ref, v_ref, wa_ref, gl_ref, rowp_ref, smallp_ref, lora_ref,
         y_ref, vout_ref, c_rkv, c_small, state, s_r, s_k, s_v, s_lw, s_wc, s_a, s_g) = refs
        dtv_ref = vf_ref = None
    L = WKV_CHUNK
    c = pl.program_id(1)
    first = c == 0
    row = lax.broadcasted_iota(jnp.int32, (L, LANES), 0)
    lane = lax.broadcasted_iota(jnp.int32, (L, LANES), 1)

    @pl.when(first)
    def _():
        state[...] = jnp.zeros_like(state)
        c_rkv[...] = jnp.zeros_like(c_rkv)
        c_small[...] = jnp.zeros_like(c_small)

    def shift_mix(src, bb, carry_ref, idx, mu):
        u = src[bb].astype(F32)
        row0 = lax.broadcasted_iota(jnp.int32, u.shape, 0) == 0
        prev = jnp.where(row0, carry_ref[bb, idx, 0:1], pltpu.roll(u, 1, 0))
        carry_ref[bb, idx, 0:1] = u[L - 1:L]
        return u + (prev - u) * mu

    rowf = lambda slot: rowp_ref[slot:slot + 1]
    tril = (lax.broadcasted_iota(jnp.int32, (L, L), 0)
            >= lax.broadcasted_iota(jnp.int32, (L, L), 1)).astype(BF16)
    def pre_steps(bb):
        s_r[bb] = shift_mix(r_ref, bb, c_rkv, 0, rowf(RP_MU_R))
        s_k[bb] = shift_mix(k_ref, bb, c_rkv, 1, rowf(RP_MU_K))
        xv = shift_mix(v_ref, bb, c_rkv, 2, rowf(RP_MU_V))
        xwa = shift_mix(wa_ref, bb, c_small, 0, smallp_ref[SP_MU_WA:SP_MU_WA + 1])
        xg = shift_mix(gl_ref, bb, c_small, 1, smallp_ref[SP_MU_G:SP_MU_G + 1])
        if has_vres:
            xvl = shift_mix(dtv_ref, bb, c_small, 2, smallp_ref[SP_MU_VL:SP_MU_VL + 1])
            mix = _sigmoid(rowf(RP_V0) + _dot(xvl.astype(BF16), lora_ref[LR_VAL]))
            xv = xv + (vf_ref[bb].astype(F32) - xv) * mix
        else:
            vout_ref[bb] = xv.astype(vout_ref.dtype)
        s_v[bb] = xv
        w_in = rowf(RP_W0) + _dot(jnp.tanh(xwa).astype(BF16), lora_ref[LR_DECAY])
        a_in = rowf(RP_A0) + _dot(xwa.astype(BF16), lora_ref[LR_ICLR])
        s_g[bb] = _dot(_sigmoid(xg).astype(BF16), lora_ref[LR_GATE])
        yield
        lw_all = -LOG2_E * jnp.exp(-_softplus(-w_in) - 0.5)
        s_lw[bb] = lw_all
        lw_hi, lw_lo = _split_hi_lo(lw_all)
        s_wc[bb] = _dot(tril, lw_hi) + _dot(tril, lw_lo)
        s_a[bb] = _sigmoid(a_in)
        yield

    m0 = lane < RWKV_HEAD
    s_in = lane % RWKV_HEAD
    strict = s_in < row
    incl = s_in <= row
    lane2 = lax.broadcasted_iota(jnp.int32, (2 * L, LANES), 1)
    row2 = lax.broadcasted_iota(jnp.int32, (2 * L, LANES), 0)
    bd_mask = (lane2 // RWKV_HEAD) == (row2 // RWKV_HEAD)
    bd_ones = bd_mask.astype(BF16)
    lane4 = lax.broadcasted_iota(jnp.int32, (4 * L, 2 * LANES), 1)
    row4 = lax.broadcasted_iota(jnp.int32, (4 * L, 2 * LANES), 0)
    bd_ones2 = ((lane4 // RWKV_HEAD) == (row4 // RWKV_HEAD)).astype(BF16)
    HB = L // 2
    lane_h = lax.broadcasted_iota(jnp.int32, (HB, LANES), 1)
    row_h = lax.broadcasted_iota(jnp.int32, (HB, LANES), 0)
    second_half = (lane_h % RWKV_HEAD) >= HB
    m0_h = lane_h < RWKV_HEAD
    eye_h = ((lane_h % HB) == row_h).astype(F32)
    blk_h = lane_h // HB
    zeros_h = jnp.zeros((HB, LANES), F32)

    def bd4(xh):
        return jnp.concatenate([jnp.where(blk_h == b, xh, 0.0) for b in range(4)], axis=0).astype(BF16)

    def rows_for_half(xh, half):
        parts = [jnp.where(m0_h, xh, 0.0), zeros_h, jnp.where(m0_h, 0.0, xh), zeros_h]
        if half == 1:
            parts = [zeros_h, parts[0], zeros_h, parts[2]]
        return jnp.concatenate(parts, axis=0).astype(BF16)

    def pair_steps(bb, p):
        sl = slice(p * LANES, (p + 1) * LANES)
        rp = lambda slot: rowp_ref[slot:slot + 1, sl]
        r = s_r[bb, :, sl]
        xk = s_k[bb, :, sl]
        v = s_v[bb, :, sl]
        lw = s_lw[bb, :, sl]
        w_cum = s_wc[bb, :, sl]
        a = s_a[bb, :, sl]
        kkr = xk * rp(RP_KK)
        k2 = xk * (1.0 + (a - 1.0) * rp(RP_KA))
        sums = _dot(jnp.concatenate([kkr * kkr, r * k2 * rp(RP_RK)], axis=1).astype(BF16), bd_ones2)
        yield
        kkn = kkr / jnp.maximum(jnp.sqrt(sums[:, 0:LANES]), 1e-12)
        bonus_s = sums[:, LANES:2 * LANES]
        avec = -kkn
        bvec = kkn * a
        w_mid = w_cum[L // 2 - 1:L // 2]
        w_end = w_cum[L - 1:L]
        e_abs = jnp.exp2(w_cum)
        e_prev = jnp.exp2(w_cum - lw)
        e_from_mid = jnp.exp2(w_mid - w_cum)
        e_to_end = jnp.exp2(w_end - w_cum)
        e_mid_inv = jnp.exp2(-w_mid)
        r_abs = r * e_abs
        a_abs = avec * e_prev
        r_mid = r_abs * e_mid_inv
        a_mid = a_abs * e_mid_inv
        b_mid = bvec * e_from_mid
        k_mid = k2 * e_from_mid
        b_end = bvec * e_to_end
        k_end = k2 * e_to_end

        st = state[bb * N_PAIRS + p]
        lhs_abs = jnp.concatenate([a_abs, r_abs], axis=0).astype(BF16)
        m1 = _dot_nt(lhs_abs, st.astype(BF16))
        lhs_mid = jnp.concatenate([a_mid, r_mid], axis=0).astype(BF16)
        rhs_mid = jnp.concatenate([_bd_stack(b_mid, m0), _bd_stack(k_mid, m0)], axis=0).astype(BF16)
        m2 = _dot_nt(lhs_mid, rhs_mid)
        yield
        a_ab = jnp.where(strict, m2[0:L, 0:LANES], 0.0)
        a_ak = jnp.where(strict, m2[0:L, LANES:2 * LANES], 0.0)
        a_rb = jnp.where(incl, m2[L:2 * L, 0:LANES], 0.0)
        a_rk = jnp.where(incl, m2[L:2 * L, LANES:2 * LANES], 0.0)

        v_bd = _bd_stack(v, m0).astype(BF16)
        rhs_u = m1[0:L] + _dot(a_ak.astype(BF16), v_bd)

        top = a_ab[0:HB]
        bot = a_ab[HB:L]
        diag = jnp.where(second_half, bot, top)
        a21 = jnp.where(second_half, 0.0, bot)
        x_inv = eye_h + diag
        pw = _dot(diag.astype(BF16), bd4(diag))
        yield
        for _ in range(3):
            res = _dot(jnp.concatenate([x_inv, pw], axis=0).astype(BF16), bd4(pw))
            yield
            x_inv = x_inv + res[0:HB]
            pw = res[HB:L]
        x_upd = _dot(x_inv.astype(BF16), bd4(pw))
        yield
        x_inv_bf = (x_inv + x_upd).astype(BF16)
        u1 = _dot(x_inv_bf, rows_for_half(rhs_u[0:HB], 0))
        yield
        rhs2 = rhs_u[HB:L] + _dot(a21.astype(BF16), rows_for_half(u1, 0))
        yield
        u2 = _dot(x_inv_bf, rows_for_half(rhs2, 1))
        yield
        u = jnp.concatenate([u1, u2], axis=0)
        u_bd = _bd_stack(u, m0).astype(BF16)
        y = m1[L:2 * L] + _dot(jnp.concatenate([a_rb, a_rk], axis=1).astype(BF16),
                               jnp.concatenate([u_bd, v_bd], axis=0))

        uv_t = jnp.concatenate([u, v], axis=0).T.astype(BF16)
        upd = _dot(uv_t, jnp.concatenate([b_end, k_end], axis=0).astype(BF16))
        yield
        state[bb * N_PAIRS + p] = st * jnp.exp2(w_end) + jnp.where(bd_mask, upd, 0.0)

        inv_n = 1.0 / RWKV_HEAD
        mean = _dot(y.astype(BF16), bd_ones) * inv_n
        yield
        yc = y - mean
        var = _dot((yc * yc).astype(BF16), bd_ones) * inv_n
        yield
        yn = yc * lax.rsqrt(var + RWKV_GN_EPS) * rp(RP_LNG) + rp(RP_LNB)
        y_ref[bb, :, sl] = ((yn + bonus_s * v) * s_g[bb, :, sl]).astype(y_ref.dtype)

    chains = []
    started = 0
    pre = pre_steps(0)
    while chains or pre is not None:
        if pre is not None:
            try:
                next(pre)
            except StopIteration:
                chains = chains + [pair_steps(started, p) for p in range(N_PAIRS)]
                started += 1
                pre = pre_steps(started) if started < nb else None
        alive = []
        for chain in chains:
            try:
                next(chain)
                alive.append(chain)
            except StopIteration:
                pass
        chains = alive


def _rwkv_branch(proj3, v_first, rowp, smallp, lora, nb):
    b, s, _ = proj3.shape
    L = WKV_CHUNK
    has_vres = v_first is not None
    assert b % nb == 0 and s % L == 0, (b, s)
    wide = lambda col: pl.BlockSpec((nb, L, D_MODEL), lambda i, j, c=col // D_MODEL: (i, j, c))
    narrow = lambda col: pl.BlockSpec((nb, L, LANES), lambda i, j, c=col // LANES: (i, j, c))
    full = lambda shape: pl.BlockSpec(shape, lambda i, j: (0,) * len(shape))
    out_blk = pl.BlockSpec((nb, L, D_MODEL), lambda i, j: (i, j, 0))
    in_specs = [wide(COL_R), wide(COL_K), wide(COL_V), narrow(COL_WA), narrow(COL_GLO)]
    args = [proj3] * 5
    if has_vres:
        in_specs += [narrow(COL_DTV), out_blk]
        args += [proj3, v_first]
    in_specs += [full(rowp.shape), full(smallp.shape), full(lora.shape)]
    args += [rowp, smallp, lora]
    wide_scr = lambda: pltpu.VMEM((nb, L, D_MODEL), F32)
    scratch = [pltpu.VMEM((nb, 3, SUBLANES, D_MODEL), F32), pltpu.VMEM((nb, 3, SUBLANES, LANES), F32),
               pltpu.VMEM((nb * N_PAIRS, 2 * RWKV_HEAD, LANES), F32)] + [wide_scr() for _ in range(7)]
    y_shape = jax.ShapeDtypeStruct((b, s, D_MODEL), BF16)
    out_shape = y_shape if has_vres else (y_shape, y_shape)
    out_specs = out_blk if has_vres else (out_blk, out_blk)
    return pl.pallas_call(
        functools.partial(_rwkv_kernel, has_vres, nb),
        out_shape=out_shape,
        grid=(b // nb, s // L),
        in_specs=in_specs,
        out_specs=out_specs,
        scratch_shapes=scratch,
        compiler_params=pltpu.CompilerParams(
            dimension_semantics=("arbitrary", "arbitrary"), vmem_limit_bytes=VMEM_LIMIT),
        name="rwkv7",
    )(*args)


HEADS_PER_GROUP = MAMBA_HEADS // MAMBA_GROUPS
GROUP_WIDTH = MAMBA_INNER // MAMBA_GROUPS
BC_WIDTH = 2 * MAMBA_GROUPS * MAMBA_STATE
DP_DT_BIAS, DP_A_HEAD = range(2)
LOG2_E = 1.4426950408889634


def _silu(x):
    half = 0.5 * x
    return half + half * jnp.tanh(half)


def _mamba_kernel(z_ref, xs_ref, bc_ref, dtv_ref, cw_xs_ref, cb_xs_ref, cw_bc_ref, cb_bc_ref, dtp_ref,
                  dskip_ref, mnorm_ref, expand_ref, y_ref, tail_xs, tail_bc, state):
    L = SSD_CHUNK
    c = pl.program_id(1)
    first = c == 0

    @pl.when(first)
    def _():
        state[...] = jnp.zeros_like(state)
        tail_xs[...] = jnp.zeros_like(tail_xs)
        tail_bc[...] = jnp.zeros_like(tail_bc)

    def conv_silu(src_ref, tail_ref, w_ref, b_ref, cols):
        u = src_ref[0, :, cols].astype(F32)
        row8 = lax.broadcasted_iota(jnp.int32, (SUBLANES, u.shape[1]), 0)
        tail = tail_ref[:, cols]
        acc = u * w_ref[MAMBA_CONV - 1:MAMBA_CONV, cols] + b_ref[:, cols]
        for d in range(1, MAMBA_CONV):
            rolled = pltpu.roll(u, d, 0)
            head = jnp.where(row8 < d, pltpu.roll(tail, d, 0), rolled[0:SUBLANES])
            shifted = jnp.concatenate([head, rolled[SUBLANES:]], axis=0)
            acc = acc + shifted * w_ref[MAMBA_CONV - 1 - d:MAMBA_CONV - d, cols]
        tail_ref[:, cols] = u[L - SUBLANES:L]
        return _silu(acc)

    dt = _softplus(dtv_ref[0].astype(F32) + dtp_ref[DP_DT_BIAS:DP_DT_BIAS + 1])
    adt = dt * dtp_ref[DP_A_HEAD:DP_A_HEAD + 1]
    tril_b = lax.broadcasted_iota(jnp.int32, (L, L), 0) >= lax.broadcasted_iota(jnp.int32, (L, L), 1)
    tril = tril_b.astype(BF16)
    adt_hi, adt_lo = _split_hi_lo(adt)
    a_cs = _dot(tril, adt_hi) + _dot(tril, adt_lo)
    a_cs2 = a_cs * LOG2_E
    a_cs2_t = a_cs2.T
    dt_bf = dt.astype(BF16)
    exp_acs_bf = jnp.exp(a_cs).astype(BF16)
    to_end_bf = jnp.exp(a_cs[L - 1:L] - a_cs).astype(BF16)
    lane = lax.broadcasted_iota(jnp.int32, (L, LANES), 1)
    m0 = lane < MAMBA_HEAD

    for g in range(MAMBA_GROUPS):
        gs = slice(g * GROUP_WIDTH, (g + 1) * GROUP_WIDTH)
        b_cols = slice(g * MAMBA_STATE, (g + 1) * MAMBA_STATE)
        c_cols = slice((MAMBA_GROUPS + g) * MAMBA_STATE, (MAMBA_GROUPS + g + 1) * MAMBA_STATE)
        xc = conv_silu(xs_ref, tail_xs, cw_xs_ref, cb_xs_ref, gs)
        bg = conv_silu(bc_ref, tail_bc, cw_bc_ref, cb_bc_ref, b_cols)
        cg_bf = conv_silu(bc_ref, tail_bc, cw_bc_ref, cb_bc_ref, c_cols).astype(BF16)
        expand = expand_ref[:, gs]
        exp_acs_x = _dot(exp_acs_bf, expand)
        xdt = xc * _dot(dt_bf, expand)
        xdt_bf = xdt.astype(BF16)
        xw_bf = (xdt * _dot(to_end_bf, expand)).astype(BF16)
        cb = jnp.where(tril_b, _dot_nt(cg_bf, bg.astype(BF16)), 0.0)
        st = state[g]
        y_off = _dot(cg_bf, st.astype(BF16)) * exp_acs_x
        y_parts = []
        for jp in range(HEADS_PER_GROUP // 2):
            ws = []
            for hh in range(2):
                j = g * HEADS_PER_GROUP + 2 * jp + hh
                seg2 = a_cs2[:, j:j + 1] - a_cs2_t[j:j + 1, :]
                ws.append((cb * jnp.exp2(jnp.minimum(seg2, 0.0))).astype(BF16))
            ps = slice(jp * LANES, (jp + 1) * LANES)
            y_parts.append(_dot(jnp.concatenate(ws, axis=1), _bd_stack(xdt_bf[:, ps], m0)))
        state[g] = st * exp_acs_x[L - 1:L] + _dot(bg.T.astype(BF16), xw_bf)
        y = jnp.concatenate(y_parts, axis=1) + y_off + xc * dskip_ref[:, gs]
        y = y * _silu(z_ref[0, :, gs].astype(F32))
        ms = jnp.mean(y * y, axis=-1, keepdims=True)
        y_ref[0, :, gs] = (y * lax.rsqrt(ms + MAMBA_NORM_EPS) * mnorm_ref[:, gs]).astype(y_ref.dtype)


def _mamba_branch(proj3, cw_xs, cb_xs, cw_bc, cb_bc, dtp, dskip_x, mnorm, expand):
    b, s, _ = proj3.shape
    L = SSD_CHUNK
    blk = lambda width, col: pl.BlockSpec((1, L, width), lambda i, j, c=col // width: (i, j, c))
    full = lambda arr: pl.BlockSpec(arr.shape, lambda i, j: (0,) * arr.ndim)
    params = [cw_xs, cb_xs, cw_bc, cb_bc, dtp, dskip_x, mnorm, expand]
    return pl.pallas_call(
        _mamba_kernel,
        out_shape=jax.ShapeDtypeStruct((b, s, MAMBA_INNER), BF16),
        grid=(b, s // L),
        in_specs=[blk(MAMBA_INNER, COL_Z), blk(MAMBA_INNER, COL_XS), blk(BC_WIDTH, COL_BC),
                  blk(LANES, COL_DTV)] + [full(a) for a in params],
        out_specs=pl.BlockSpec((1, L, MAMBA_INNER), lambda i, j: (i, j, 0)),
        scratch_shapes=[pltpu.VMEM((SUBLANES, MAMBA_INNER), F32), pltpu.VMEM((SUBLANES, BC_WIDTH), F32),
                        pltpu.VMEM((MAMBA_GROUPS, MAMBA_STATE, GROUP_WIDTH), F32)],
        compiler_params=pltpu.CompilerParams(
            dimension_semantics=("arbitrary", "arbitrary"), vmem_limit_bytes=VMEM_LIMIT),
        name="mamba2_ssd",
    )(proj3, proj3, proj3, proj3, *params)


ROUTE_E0 = N_EXPERT_GROUPS
RI_E1, RI_E2, RI_RANK1, RI_RANK2 = range(4)
BIG = 1e30


def _merge_kernel(ya_ref, yb_ref, gate_ref, x_ref, wro_ref, wmo_ref, wout_ref, g2_ref, wr_ref, br_ref,
                  xo_ref, h2_ref, ri_ref, rw_ref, cnt_ref, run_cnt):
    tm = x_ref.shape[0]
    i = pl.program_id(0)

    @pl.when(i == 0)
    def _():
        run_cnt[...] = jnp.zeros_like(run_cnt)

    o_a = _dot(ya_ref[...], wro_ref[...])
    o_b = _dot(yb_ref[...], wmo_ref[...])
    gates = _sigmoid(gate_ref[...].astype(F32))
    merged = gates[:, 0:D_MODEL] * o_a + gates[:, D_MODEL:2 * D_MODEL] * o_b
    x = x_ref[...] + _dot(merged.astype(BF16), wout_ref[...])
    xo_ref[...] = x
    ms = jnp.mean(x * x, axis=-1, keepdims=True)
    h2 = x * lax.rsqrt(ms + NORM_EPS) * g2_ref[...]
    h2_ref[...] = h2

    logits = _dot(h2.astype(BF16), wr_ref[...]) + br_ref[...]
    lane = lax.broadcasted_iota(jnp.int32, (tm, LANES), 1)
    gmask = lane < N_EXPERT_GROUPS
    gl = jnp.where(gmask, logits, -BIG)
    gmax = jnp.max(gl, axis=-1, keepdims=True)
    gidx = jnp.min(jnp.where(gl == gmax, lane, LANES), axis=-1, keepdims=True)
    gsum = jnp.sum(jnp.where(gmask, jnp.exp(gl - gmax), 0.0), axis=-1, keepdims=True)
    g_p = 1.0 / gsum
    e_lo = ROUTE_E0 + gidx * EXPERTS_PER_GROUP
    emask = (lane >= e_lo) & (lane < e_lo + EXPERTS_PER_GROUP)
    el = jnp.where(emask, logits, -BIG)
    e1 = jnp.max(el, axis=-1, keepdims=True)
    i1 = jnp.min(jnp.where(el == e1, lane, LANES), axis=-1, keepdims=True)
    el2 = jnp.where(lane == i1, -BIG, el)
    e2 = jnp.max(el2, axis=-1, keepdims=True)
    i2 = jnp.min(jnp.where(el2 == e2, lane, LANES), axis=-1, keepdims=True)
    q = jnp.exp(e2 - e1)
    w1 = g_p / (1.0 + q)
    w2 = g_p * q / (1.0 + q)
    eid1 = i1 - ROUTE_E0
    eid2 = i2 - ROUTE_E0

    onehot = ((lane == eid1) | (lane == eid2)).astype(BF16)
    strict = (lax.broadcasted_iota(jnp.int32, (tm, tm), 0)
              > lax.broadcasted_iota(jnp.int32, (tm, tm), 1)).astype(BF16)
    before = _dot(strict, onehot) + run_cnt[0:1]
    rank1 = jnp.sum(jnp.where(lane == eid1, before, 0.0), axis=-1, keepdims=True).astype(jnp.int32)
    rank2 = jnp.sum(jnp.where(lane == eid2, before, 0.0), axis=-1, keepdims=True).astype(jnp.int32)
    total = run_cnt[0:1] + jnp.sum(onehot.astype(F32), axis=0, keepdims=True)
    run_cnt[0:1] = total
    cnt_ref[...] = jnp.broadcast_to(total, cnt_ref.shape)

    ri = jnp.where(lane == RI_E1, eid1, 0)
    ri = jnp.where(lane == RI_E2, eid2, ri)
    ri = jnp.where(lane == RI_RANK1, rank1, ri)
    ri = jnp.where(lane == RI_RANK2, rank2, ri)
    ri_ref[...] = ri
    rw_ref[...] = jnp.where(lane == 0, w1, jnp.where(lane == 1, w2, 0.0))


def _merge(ya, yb, proj, x2d, w_ro, w_mo, w_out, g2_row, w_r, b_r, tm):
    t = x2d.shape[0]
    full = lambda arr: pl.BlockSpec(arr.shape, lambda i: (0,) * arr.ndim, pipeline_mode=pl.Buffered(1))
    rows = lambda width: pl.BlockSpec((tm, width), lambda i: (i, 0))
    return pl.pallas_call(
        _merge_kernel,
        out_shape=(jax.ShapeDtypeStruct((t, D_MODEL), F32),
                   jax.ShapeDtypeStruct((t, D_MODEL), F32),
                   jax.ShapeDtypeStruct((t, LANES), jnp.int32),
                   jax.ShapeDtypeStruct((t, LANES), F32),
                   jax.ShapeDtypeStruct((SUBLANES, LANES), F32)),
        grid=(t // tm,),
        in_specs=[rows(D_MODEL), rows(MAMBA_INNER),
                  pl.BlockSpec((tm, 2 * D_MODEL), lambda i: (i, COL_GATE // (2 * D_MODEL))),
                  rows(D_MODEL), full(w_ro), full(w_mo), full(w_out), full(g2_row), full(w_r), full(b_r)],
        out_specs=(rows(D_MODEL), rows(D_MODEL),
                   rows(LANES), rows(LANES), pl.BlockSpec((SUBLANES, LANES), lambda i: (0, 0))),
        scratch_shapes=[pltpu.VMEM((SUBLANES, LANES), F32)],
        compiler_params=pltpu.CompilerParams(
            dimension_semantics=("arbitrary",), vmem_limit_bytes=VMEM_LIMIT),
        name="merge_route",
    )(ya, yb, proj, x2d, w_ro, w_mo, w_out, g2_row, w_r, b_r)


def _fetch_step_indices(pos_hbm, idx_smem, idx_sem):
    i = pl.program_id(0)
    n = pl.num_programs(0)
    slot = i % 2
    per_step = pos_hbm.shape[1]

    def copy(step, s):
        dst = idx_smem.at[pl.ds(pl.multiple_of(s * per_step, per_step), per_step)]
        return pltpu.make_async_copy(pos_hbm.at[step], dst, idx_sem.at[s])

    @pl.when(i == 0)
    def _():
        copy(0, 0).start()

    copy(i, slot).wait()

    @pl.when(i + 1 < n)
    def _():
        copy(i + 1, 1 - slot).start()

    return slot * per_step


def _issue_row_copies(row_copy, n_groups):
    def issue(g, carry):
        for u in range(SUBLANES):
            for k in range(2):
                row_copy(g, u, k).start(priority=k)
        return carry

    lax.fori_loop(0, n_groups, issue, 0)


META_PEND, META_PCOUNT, META_N_USED = 0, N_EXPERTS, 2 * N_EXPERTS
META_SIZE = 2 * N_EXPERTS + 1


def _dispatch_kernel(min_used_blocks, meta_ref, pos_hbm, h2_ref, xb_out, idx_smem, zero_blk, idx_sem, row_sem,
                     zero_sem):
    n_groups = h2_ref.shape[0]
    tm = n_groups * SUBLANES
    n_blocks = xb_out.shape[0] // EXPERT_ROWS
    n_used = meta_ref[META_N_USED]

    @pl.when(pl.program_id(0) == 0)
    def _():
        zero_blk[...] = jnp.zeros_like(zero_blk)

        def zero_fill(start):
            dst = xb_out.at[pl.ds(pl.multiple_of(start, EXPERT_ROWS), EXPERT_ROWS)]
            return pltpu.make_async_copy(zero_blk, dst, zero_sem)

        fills = [(meta_ref[META_PCOUNT + e] > 0, meta_ref[META_PEND + e] - EXPERT_ROWS)
                 for e in range(N_EXPERTS)]
        fills += [(blk >= n_used, blk * EXPERT_ROWS) for blk in range(min_used_blocks, n_blocks)]
        for pred, start in fills:
            @pl.when(pred)
            def _():
                zero_fill(start).start()
        for pred, start in fills:
            @pl.when(pred)
            def _():
                zero_fill(start).wait()

    base = _fetch_step_indices(pos_hbm, idx_smem, idx_sem)

    def row_copy(g, u, k):
        row = idx_smem[base + 2 * SUBLANES * g + (2 * u + k)]
        return pltpu.make_async_copy(h2_ref.at[g, pl.ds(u, 1)], xb_out.at[pl.ds(row, 1)], row_sem)

    _issue_row_copies(row_copy, n_groups)
    for _ in range(2):
        pltpu.make_async_copy(xb_out.at[pl.ds(0, tm)], xb_out.at[pl.ds(0, tm)], row_sem).wait()


def _dispatch(meta, pos2, h2, n_rows, tm):
    t = h2.shape[0]
    min_used_blocks = (2 * t) // EXPERT_ROWS
    grid_spec = pltpu.PrefetchScalarGridSpec(
        num_scalar_prefetch=1,
        grid=(t // tm,),
        in_specs=[pl.BlockSpec(memory_space=pl.ANY),
                  pl.BlockSpec((tm // SUBLANES, SUBLANES, D_MODEL), lambda i, meta: (i, 0, 0))],
        out_specs=pl.BlockSpec(memory_space=pl.ANY),
        scratch_shapes=[pltpu.SMEM((4 * tm,), jnp.int32), pltpu.VMEM((EXPERT_ROWS, D_MODEL), F32),
                        pltpu.SemaphoreType.DMA((2,)), pltpu.SemaphoreType.DMA, pltpu.SemaphoreType.DMA],
    )
    return pl.pallas_call(
        functools.partial(_dispatch_kernel, min_used_blocks),
        out_shape=jax.ShapeDtypeStruct((n_rows, D_MODEL), F32),
        grid_spec=grid_spec,
        compiler_params=pltpu.CompilerParams(
            dimension_semantics=("arbitrary",), vmem_limit_bytes=VMEM_LIMIT),
        name="moe_dispatch",
    )(meta, pos2, h2.reshape(t // SUBLANES, SUBLANES, D_MODEL))


def _ffn_kernel(be_ref, xb_ref, w1_ref, w3_ref, w2_ref, yb_ref, w1_bf, w3_bf, w2_bf):
    i = pl.program_id(0)
    n_used = be_ref[pl.num_programs(0)]

    @pl.when((i == 0) | (be_ref[i] != be_ref[jnp.maximum(i - 1, 0)]))
    def _():
        w1_bf[...] = w1_ref[0, 0].astype(BF16)
        w3_bf[...] = w3_ref[0, 0].astype(BF16)
        w2_bf[...] = w2_ref[0, 0].astype(BF16)

    @pl.when(i < n_used)
    def _():
        x = xb_ref[...].astype(BF16)
        h1 = _dot(x, w1_bf[...])
        h3 = _dot(x, w3_bf[...])
        hid = (_silu(h1) * h3).astype(BF16)
        yb_ref[...] = _dot(hid, w2_bf[...])

    @pl.when(i >= n_used)
    def _():
        yb_ref[...] = jnp.zeros_like(yb_ref)


def _expert_ffn(block_e, xb, w1, w3, w2, layer):
    n_rows = xb.shape[0]
    br = EXPERT_ROWS
    grid_spec = pltpu.PrefetchScalarGridSpec(
        num_scalar_prefetch=1,
        grid=(n_rows // br,),
        in_specs=[pl.BlockSpec((br, D_MODEL), lambda i, be: (i, 0)),
                  pl.BlockSpec((1, 1, D_MODEL, EXPERT_FF), lambda i, be: (layer, be[i], 0, 0)),
                  pl.BlockSpec((1, 1, D_MODEL, EXPERT_FF), lambda i, be: (layer, be[i], 0, 0)),
                  pl.BlockSpec((1, 1, EXPERT_FF, D_MODEL), lambda i, be: (layer, be[i], 0, 0))],
        out_specs=pl.BlockSpec((br, D_MODEL), lambda i, be: (i, 0)),
        scratch_shapes=[pltpu.VMEM((D_MODEL, EXPERT_FF), BF16), pltpu.VMEM((D_MODEL, EXPERT_FF), BF16),
                        pltpu.VMEM((EXPERT_FF, D_MODEL), BF16)],
    )
    return pl.pallas_call(
        _ffn_kernel,
        out_shape=jax.ShapeDtypeStruct(xb.shape, F32),
        grid_spec=grid_spec,
        compiler_params=pltpu.CompilerParams(
            dimension_semantics=("arbitrary",), vmem_limit_bytes=VMEM_LIMIT),
        name="expert_ffn",
    )(block_e, xb, w1, w3, w2)


def _combine_kernel(final_norm, pos_hbm, yb_hbm, x_ref, rw_ref, fg_ref, o_ref, idx_smem, buf, idx_sem, row_sem):
    tm = x_ref.shape[0]
    n_groups = tm // SUBLANES
    base = _fetch_step_indices(pos_hbm, idx_smem, idx_sem)

    def row_copy(g, u, k):
        row = idx_smem[base + 2 * SUBLANES * g + (2 * u + k)]
        return pltpu.make_async_copy(yb_hbm.at[pl.ds(row, 1)], buf.at[k, g, pl.ds(u, 1)], row_sem)

    _issue_row_copies(row_copy, n_groups)
    for _ in range(2):
        pltpu.make_async_copy(yb_hbm.at[pl.ds(0, tm)], yb_hbm.at[pl.ds(0, tm)], row_sem).wait()

    rw = rw_ref[...]
    y1 = buf[0].reshape(tm, D_MODEL)
    y2 = buf[1].reshape(tm, D_MODEL)
    x = x_ref[...] + rw[:, 0:1] * y1 + rw[:, 1:2] * y2
    if final_norm:
        ms = jnp.mean(x * x, axis=-1, keepdims=True)
        x = x * lax.rsqrt(ms + NORM_EPS) * fg_ref[...]
    o_ref[...] = x


def _combine(pos2, yb, x2d, rw, final_g_row, final_norm, tm):
    t = x2d.shape[0]
    return pl.pallas_call(
        functools.partial(_combine_kernel, final_norm),
        out_shape=jax.ShapeDtypeStruct(x2d.shape, F32),
        grid=(t // tm,),
        in_specs=[pl.BlockSpec(memory_space=pl.ANY), pl.BlockSpec(memory_space=pl.ANY),
                  pl.BlockSpec((tm, D_MODEL), lambda i: (i, 0)),
                  pl.BlockSpec((tm, LANES), lambda i: (i, 0)),
                  pl.BlockSpec((1, D_MODEL), lambda i: (0, 0))],
        out_specs=pl.BlockSpec((tm, D_MODEL), lambda i: (i, 0)),
        scratch_shapes=[pltpu.SMEM((4 * tm,), jnp.int32),
                        pltpu.VMEM((2, tm // SUBLANES, SUBLANES, D_MODEL), F32),
                        pltpu.SemaphoreType.DMA((2,)), pltpu.SemaphoreType.DMA],
        compiler_params=pltpu.CompilerParams(
            dimension_semantics=("arbitrary",), vmem_limit_bytes=VMEM_LIMIT),
        name="moe_combine",
    )(pos2, yb, x2d, rw, final_g_row)


def _pad_cols(w, width):
    return jnp.pad(w, ((0, 0), (0, width - w.shape[1])))


def _pack_w_in(w_in_l, w_v1_l):
    rw = 3 * D_MODEL
    o_wlo = rw
    o_alo = o_wlo + DECAY_LORA
    o_glo = o_alo + ICLR_LORA
    o_m = o_glo + GATE_LORA
    o_xbc = o_m + MAMBA_INNER
    o_dt = o_xbc + MAMBA_INNER + BC_WIDTH
    o_gate = o_dt + MAMBA_HEADS
    d = w_in_l.shape[0]
    vlo = w_v1_l if w_v1_l is not None else jnp.zeros((d, VALUE_LORA), w_in_l.dtype)
    dtv = _pad_cols(jnp.concatenate([w_in_l[:, o_dt:o_dt + MAMBA_HEADS], vlo], axis=1), LANES)
    cols = [
        w_in_l[:, o_m:o_m + MAMBA_INNER],
        w_in_l[:, o_gate:o_gate + 2 * D_MODEL],
        w_in_l[:, o_xbc:o_xbc + MAMBA_INNER],
        w_in_l[:, 0:rw],
        w_in_l[:, o_xbc + MAMBA_INNER:o_xbc + MAMBA_INNER + BC_WIDTH],
        w_in_l[:, o_wlo:o_glo],
        w_in_l[:, o_glo:o_m],
        dtv,
        jnp.zeros((d, N_COLS - COL_DTV - LANES), w_in_l.dtype),
    ]
    return jnp.concatenate(cols, axis=1).astype(BF16)


def _pack_rwkv_params(l, mu_rwkv, w0, a0, v0, k_k, k_a, r_k, lnx_g, lnx_b, w_decay2, w_iclr2, w_gate2,
                      mu_v, w_v2):
    mu = mu_rwkv[l]
    has_vres = l > 0
    zeros = jnp.zeros((D_MODEL,), F32)
    rows = [mu[0:D_MODEL], mu[D_MODEL:2 * D_MODEL], mu[2 * D_MODEL:3 * D_MODEL], w0[l], a0[l],
            v0[l - 1] if has_vres else zeros, k_k[l], k_a[l], r_k[l].reshape(-1), lnx_g[l], lnx_b[l]]
    rows += [zeros] * (RP_ROWS - len(rows))
    rowp = jnp.stack(rows, axis=0)
    rw = 3 * D_MODEL
    mu_vl = jnp.zeros((LANES,), F32)
    if has_vres:
        mu_vl = mu_vl.at[VLO_LANE:VLO_LANE + VALUE_LORA].set(mu_v[l - 1])
    smallp = jnp.stack([mu[rw:rw + LANES], mu[rw + LANES:rw + 2 * LANES], mu_vl]
                       + [jnp.zeros((LANES,), F32)] * (SUBLANES - 3), axis=0)
    z = lambda n: jnp.zeros((n, D_MODEL), F32)
    wd = jnp.concatenate([w_decay2[l], z(LANES - DECAY_LORA)], axis=0)
    wi = jnp.concatenate([z(DECAY_LORA), w_iclr2[l]], axis=0)
    wg = w_gate2[l]
    if has_vres:
        wv = jnp.concatenate([z(VLO_LANE), w_v2[l - 1], z(LANES - VLO_LANE - VALUE_LORA)], axis=0)
    else:
        wv = z(LANES)
    lora = jnp.stack([wd, wi, wg, wv], axis=0)
    return rowp, smallp, lora.astype(BF16)


def _pack_mamba_params(l, conv_w, conv_b, dt_bias, a_log, d_skip, mnorm_g):
    cw = conv_w[l].astype(F32)
    cb = conv_b[l].reshape(1, -1)
    cw_xs = jnp.concatenate([cw[:, :MAMBA_INNER], jnp.zeros((SUBLANES - MAMBA_CONV, MAMBA_INNER), F32)], axis=0)
    cw_bc = jnp.concatenate([cw[:, MAMBA_INNER:], jnp.zeros((SUBLANES - MAMBA_CONV, BC_WIDTH), F32)], axis=0)
    pad = lambda v: jnp.pad(v, (0, LANES - MAMBA_HEADS))
    dtp = jnp.stack([pad(dt_bias[l]), pad(-jnp.exp(a_log[l].astype(F32)))]
                    + [jnp.zeros((LANES,), F32)] * (SUBLANES - 2), axis=0)
    dskip_x = jnp.repeat(d_skip[l], MAMBA_HEAD).reshape(1, MAMBA_INNER)
    expand = (jnp.arange(LANES)[:, None] == (jnp.arange(MAMBA_INNER)[None, :] // MAMBA_HEAD)).astype(BF16)
    return cw_xs, cb[:, :MAMBA_INNER], cw_bc, cb[:, MAMBA_INNER:], dtp, dskip_x, mnorm_g[l].reshape(1, -1), expand


def _route_positions(ri, cnt, n_blocks):
    counts = cnt[0, :N_EXPERTS].astype(jnp.int32)
    pcounts = (counts + EXPERT_ROWS - 1) // EXPERT_ROWS * EXPERT_ROWS
    pends = jnp.cumsum(pcounts)
    poffsets = pends - pcounts
    eid = ri[:, RI_E1:RI_E2 + 1]
    onehot = eid[:, :, None] == jnp.arange(N_EXPERTS, dtype=jnp.int32)
    pos = jnp.sum(jnp.where(onehot, poffsets, 0), axis=-1) + ri[:, RI_RANK1:RI_RANK2 + 1]
    n_used = pends[N_EXPERTS - 1:] // EXPERT_ROWS
    meta = jnp.concatenate([pends, pcounts, n_used]).astype(jnp.int32)
    block_start = jnp.arange(n_blocks, dtype=jnp.int32) * EXPERT_ROWS
    block_e = jnp.minimum(jnp.sum(block_start[:, None] >= pends[None, :], axis=1), N_EXPERTS - 1)
    block_e = jnp.concatenate([block_e.astype(jnp.int32), n_used.astype(jnp.int32)])
    return pos, meta, block_e


def kernel(x, norm1_g, w_in, mu_rwkv, w0, w_decay2, a0, w_iclr2, w_gate2, k_k, k_a, r_k, lnx_g, lnx_b, w_rwkv_o, w_v1, mu_v, v0, w_v2, conv_w, conv_b, dt_bias, a_log, d_skip, mnorm_g, w_mamba_o, w_out, norm2_g, w_rg, b_rg, w_re, b_re, w_e1, w_e3, w_e2, final_g):
    bsz, seq, d = x.shape
    t = bsz * seq
    depth = w_in.shape[0]
    n_assign = 2 * t
    n_blocks = -(-(n_assign + N_EXPERTS * (EXPERT_ROWS - 1)) // EXPERT_ROWS)
    n_rows = n_blocks * EXPERT_ROWS
    tm_proj = min(INPROJ_TM, t)
    tm_merge = min(MERGE_TM, t)
    tm_moe = min(MOE_TM, t)

    x2d = x.reshape(t, d)
    v_first = None
    for l in range(depth):
        w_cat = _pack_w_in(w_in[l], w_v1[l - 1] if l > 0 else None)
        proj = _inproj(x2d, norm1_g[l].reshape(1, d), w_cat, tm_proj, INPROJ_TN)
        proj3 = proj.reshape(bsz, seq, N_COLS)
        rowp, smallp, lora = _pack_rwkv_params(l, mu_rwkv, w0, a0, v0, k_k, k_a, r_k, lnx_g, lnx_b,
                                               w_decay2, w_iclr2, w_gate2, mu_v, w_v2)
        if l == 0:
            ya, v_first = _rwkv_branch(proj3, None, rowp, smallp, lora, WKV_BATCH)
        else:
            ya = _rwkv_branch(proj3, v_first, rowp, smallp, lora, WKV_BATCH)
        yb = _mamba_branch(proj3, *_pack_mamba_params(l, conv_w, conv_b, dt_bias, a_log, d_skip, mnorm_g))

        w_r = _pad_cols(jnp.concatenate([w_rg[l], w_re[l]], axis=1), LANES).astype(BF16)
        b_r = _pad_cols(jnp.concatenate([b_rg[l], b_re[l]]).reshape(1, -1), LANES)
        x2d, h2, ri, rw, cnt = _merge(
            ya.reshape(t, D_MODEL), yb.reshape(t, MAMBA_INNER), proj, x2d,
            w_rwkv_o[l].astype(BF16), w_mamba_o[l].astype(BF16), w_out[l].astype(BF16),
            norm2_g[l].reshape(1, d), w_r, b_r, tm_merge)

        pos, meta, block_e = _route_positions(ri, cnt, n_blocks)
        pos2 = pos.reshape(t // tm_moe, 2 * tm_moe)
        xb = _dispatch(meta, pos2, h2, n_rows, tm_moe)
        ybuf = _expert_ffn(block_e, xb, w_e1, w_e3, w_e2, l)
        x2d = _combine(pos2, ybuf, x2d, rw, final_g.reshape(1, d), l == depth - 1, tm_moe)
    return x2d.reshape(bsz, seq, d)
```
